```python
import functools
import jax, jax.numpy as jnp
from jax import lax
import numpy as np

D_MODEL = 2048
BATCH = 16
SEQ = 256
DEPTH = 2
DEC_BATCH = 8
DEC_SEQ = 4096
PAST_LEN = 256

GRID_W = 64
N_HEADS_A = 8
HEAD_DIM = 128
D_ATTN = N_HEADS_A * HEAD_DIM
D_CONV = D_MODEL - D_ATTN
D_IN = 4 * D_ATTN + 3 * D_CONV
WIN_R_MAX = 8
WIN_C = 16
CONV_K = 31
EPS = 1e-6

kernel_name = 'hybrid_natten_conformer_diffusion_step'


def _rmsnorm(x, g):
    x32 = x.astype(jnp.float32)
    y = x32 * lax.rsqrt(jnp.mean(x32 * x32, axis=-1, keepdims=True) + EPS)
    return (y * g.astype(jnp.float32)).astype(x.dtype)


def _layernorm(x, g, b):
    x32 = x.astype(jnp.float32)
    mu = jnp.mean(x32, axis=-1, keepdims=True)
    var = jnp.mean(jnp.square(x32 - mu), axis=-1, keepdims=True)
    y = (x32 - mu) * lax.rsqrt(var + EPS)
    return (y * g.astype(jnp.float32) + b.astype(jnp.float32)).astype(x.dtype)


def _heads(t):
    return t.reshape(t.shape[0], t.shape[1], N_HEADS_A, HEAD_DIM)


def _context_attention(q, k, v):
    scale = HEAD_DIM ** -0.5
    s = jnp.einsum('bqhd,bkhd->bhqk', q, k).astype(jnp.float32) * scale
    p = jax.nn.softmax(s, axis=-1).astype(v.dtype)
    o = jnp.einsum('bhqk,bkhd->bqhd', p, v)
    return o.reshape(o.shape[0], o.shape[1], D_ATTN)


def _neighbourhood_attention(q, k, v, k_ctx, v_ctx, rpb):
    B, T, H, Dh = q.shape
    rows = T // GRID_W
    kr = min(WIN_R_MAX, rows)
    nk = kr * WIN_C
    scale = HEAD_DIM ** -0.5
    cols = jnp.arange(GRID_W, dtype=jnp.int32)
    col_start = jnp.clip(cols - WIN_C // 2, 0, GRID_W - WIN_C)
    key_cols = col_start[:, None] + jnp.arange(WIN_C, dtype=jnp.int32)[None, :]
    dc_idx = key_cols - cols[:, None] + (WIN_C - 1)
    qg = q.reshape(B, rows, GRID_W, H, Dh)
    kg = k.reshape(B, rows, GRID_W, H, Dh)
    vg = v.reshape(B, rows, GRID_W, H, Dh)

    def row_block(i):
        rs = jnp.clip(i - kr // 2, 0, rows - kr)
        q_i = lax.dynamic_index_in_dim(qg, i, axis=1, keepdims=False)
        k_win = lax.dynamic_slice_in_dim(kg, rs, kr, axis=1)[:, :, key_cols]
        v_win = lax.dynamic_slice_in_dim(vg, rs, kr, axis=1)[:, :, key_cols]
        dr_idx = rs + jnp.arange(kr, dtype=jnp.int32) - i + (WIN_R_MAX - 1)
        bias = rpb[:, dr_idx[:, None, None], dc_idx[None, :, :]]
        bias = jnp.transpose(bias, (0, 2, 1, 3)).astype(jnp.float32)
        s_loc = jnp.einsum('bjhd,brjchd->bhjrc', q_i, k_win).astype(jnp.float32) * scale + bias[None]
        s_ctx = jnp.einsum('bjhd,bshd->bhjs', q_i, k_ctx).astype(jnp.float32) * scale
        s = jnp.concatenate([s_loc.reshape(B, H, GRID_W, nk), s_ctx], axis=-1)
        p = jax.nn.softmax(s, axis=-1).astype(v.dtype)
        p_loc = p[..., :nk].reshape(B, H, GRID_W, kr, WIN_C)
        p_ctx = p[..., nk:]
        return (jnp.einsum('bhjrc,brjchd->bjhd', p_loc, v_win)
                + jnp.einsum('bhjs,bshd->bjhd', p_ctx, v_ctx))

    o = lax.map(row_block, jnp.arange(rows, dtype=jnp.int32))
    return jnp.transpose(o, (1, 0, 2, 3, 4)).reshape(B, T, D_ATTN)


def _conformer_conv(u, u_gate, dw_w, dw_b, ln_g, ln_b, w_pw2, b_pw2):
    a = u * jax.nn.sigmoid(u_gate)
    a = lax.conv_general_dilated(a, dw_w[:, None, :], window_strides=(1,),
                                 padding=((CONV_K // 2, CONV_K // 2),),
                                 dimension_numbers=('NWC', 'WIO', 'NWC'),
                                 feature_group_count=D_CONV) + dw_b
    a = jax.nn.silu(_layernorm(a, ln_g, ln_b))
    return a @ w_pw2 + b_pw2


def _layer(x, cvec, attn_fn, norm_g, w_ada, b_ada, w_in, b_in, dw_w, dw_b,
           cln_g, cln_b, w_pw2, b_pw2, w_out):
    mod = jax.nn.silu(cvec) @ w_ada + b_ada
    shift, scl, gate = jnp.split(mod[:, None, :], 3, axis=-1)
    h = _rmsnorm(x, norm_g) * (1 + scl) + shift
    z = h @ w_in + b_in
    q, k, v, g_a, u, u_gate, g_c = jnp.split(
        z, [D_ATTN, 2 * D_ATTN, 3 * D_ATTN, 4 * D_ATTN,
            4 * D_ATTN + D_CONV, 4 * D_ATTN + 2 * D_CONV], axis=-1)
    q, k, v = _heads(q), _heads(k), _heads(v)
    attn = attn_fn(q, k, v) * jax.nn.silu(g_a)
    conv = _conformer_conv(u, u_gate, dw_w, dw_b, cln_g, cln_b, w_pw2, b_pw2) * jax.nn.silu(g_c)
    out = jnp.concatenate([attn, conv], axis=-1) @ w_out
    return x + gate * out, k, v


def setup_inputs(seed: int = 0) -> dict:
    key = jax.random.key(seed)
    ks = jax.random.split(key, 20)
    f32 = jnp.float32

    def nrm(k, shape, s):
        return jax.random.normal(k, shape, f32) * s

    return {
        'x_prompt': nrm(ks[0], (BATCH, SEQ, D_MODEL), 1.0),
        'x_sample': nrm(ks[1], (DEC_BATCH, DEC_SEQ, D_MODEL), 1.0),
        'cache_k': nrm(ks[2], (DEC_BATCH, DEPTH, PAST_LEN, N_HEADS_A, HEAD_DIM), 1.0),
        'cache_v': nrm(ks[3], (DEC_BATCH, DEPTH, PAST_LEN, N_HEADS_A, HEAD_DIM), 1.0),
        'c': nrm(ks[4], (DEC_BATCH, D_MODEL), 1.0),
        'c_ctx': nrm(ks[5], (D_MODEL,), 1.0),
        'norm_g': 1.0 + nrm(ks[6], (DEPTH, D_MODEL), 0.02),
        'w_ada': nrm(ks[7], (DEPTH, D_MODEL, 3 * D_MODEL), 0.5 * D_MODEL ** -0.5),
        'b_ada': nrm(ks[8], (DEPTH, 3 * D_MODEL), 0.02),
        'w_in': nrm(ks[9], (DEPTH, D_MODEL, D_IN), D_MODEL ** -0.5),
        'b_in': nrm(ks[10], (DEPTH, D_IN), 0.02),
        'rpb': nrm(ks[11], (DEPTH, N_HEADS_A, 2 * WIN_R_MAX - 1, 2 * WIN_C - 1), 0.1),
        'dw_w': nrm(ks[12], (DEPTH, CONV_K, D_CONV), CONV_K ** -0.5),
        'dw_b': nrm(ks[13], (DEPTH, D_CONV), 0.02),
        'cln_g': 1.0 + nrm(ks[14], (DEPTH, D_CONV), 0.02),
        'cln_b': nrm(ks[15], (DEPTH, D_CONV), 0.02),
        'w_pw2': nrm(ks[16], (DEPTH, D_CONV, D_CONV), D_CONV ** -0.5),
        'b_pw2': nrm(ks[17], (DEPTH, D_CONV), 0.02),
        'w_out': nrm(ks[18], (DEPTH, D_MODEL, D_MODEL), D_MODEL ** -0.5),
        'final_norm_g': 1.0 + nrm(ks[19], (D_MODEL,), 0.02),
    }


def reference(x_prompt, x_sample, cache_k, cache_v, c, c_ctx, norm_g, w_ada, b_ada,
              w_in, b_in, rpb, dw_w, dw_b, cln_g, cln_b, w_pw2, b_pw2, w_out,
              final_norm_g):
    xp = x_prompt
    xs = x_sample
    k_list, v_list = [], []
    for l in range(DEPTH):
        lw = (norm_g[l], w_ada[l], b_ada[l], w_in[l], b_in[l], dw_w[l], dw_b[l],
              cln_g[l], cln_b[l], w_pw2[l], b_pw2[l], w_out[l])
        xp, kp, vp = _layer(xp, c_ctx[None, :], _context_attention, *lw)
        k_list.append(kp)
        v_list.append(vp)
        na = functools.partial(_neighbourhood_attention, k_ctx=cache_k[:, l],
                               v_ctx=cache_v[:, l], rpb=rpb[l])
        xs, _, _ = _layer(xs, c, na, *lw)
    y_prompt = _rmsnorm(xp, final_norm_g)
    y_sample = _rmsnorm(xs, final_norm_g)
    new_cache_k = jnp.stack(k_list, axis=1)
    new_cache_v = jnp.stack(v_list, axis=1)
    return (y_prompt, y_sample, new_cache_k, new_cache_v)
```

```python
import functools

import numpy as np
import jax
import jax.numpy as jnp
from jax import lax
from jax.experimental import pallas as pl
from jax.experimental.pallas import tpu as pltpu

D_MODEL = 2048
DEPTH = 2
GRID_W = 64
N_HEADS_A = 8
HEAD_DIM = 128
D_ATTN = N_HEADS_A * HEAD_DIM
D_CONV = D_MODEL - D_ATTN
D_IN = 4 * D_ATTN + 3 * D_CONV
WIN_R = 8
WIN_C = 16
CONV_K = 31
EPS = 1e-6

F32 = jnp.float32
BF16 = jnp.bfloat16

VMEM_LIMIT_BYTES = 56 * 1024 * 1024
LANES = 128
SUBLANES = 8
BF16_ROWS = 16

MOD_ROWS = 16
MASKED = -1e30

COL_Q, COL_K, COL_V, COL_GA, COL_U, COL_UG, COL_GC = range(7)

Q_ROWS = 4
K_ROWS = 12


def _params(*sem):
    return pltpu.CompilerParams(dimension_semantics=sem, vmem_limit_bytes=VMEM_LIMIT_BYTES)


def _mod_kernel(c_ref, w_ref, b_ref, o_ref):
    c = c_ref[...]
    s = c * jax.nn.sigmoid(c)
    o_ref[...] = jnp.dot(s.astype(BF16), w_ref[...].astype(BF16),
                         preferred_element_type=F32) + b_ref[...]


def _modulation(cvec, w_ada, b_ada):
    tn = 1536
    return pl.pallas_call(
        _mod_kernel,
        grid=(DEPTH, 3 * D_MODEL // tn),
        in_specs=[
            pl.BlockSpec((MOD_ROWS, D_MODEL), lambda l, j: (0, 0)),
            pl.BlockSpec((None, D_MODEL, tn), lambda l, j: (l, 0, j)),
            pl.BlockSpec((None, 1, tn), lambda l, j: (l, 0, j)),
        ],
        out_specs=pl.BlockSpec((None, MOD_ROWS, tn), lambda l, j: (l, 0, j)),
        out_shape=jax.ShapeDtypeStruct((DEPTH, MOD_ROWS, 3 * D_MODEL), F32),
        compiler_params=_params("arbitrary", "arbitrary"),
        name="modulation",
    )(cvec, w_ada, b_ada.reshape(DEPTH, 1, 3 * D_MODEL))


def _inproj_kernel(x_ref, g_ref, shift_ref, scl_ref, w_ref, b_ref, *rest, emit_kv):
    if emit_kv:
        z_ref, kv_ref, h_ref = rest
    else:
        z_ref, h_ref = rest
    j = pl.program_id(1)

    @pl.when(j == 0)
    def _():
        x = x_ref[...]
        y = x * lax.rsqrt(jnp.mean(x * x, axis=-1, keepdims=True) + EPS) * g_ref[...]
        h_ref[...] = (y * (1.0 + scl_ref[...]) + shift_ref[...]).astype(BF16)

    acc = jnp.dot(h_ref[...], w_ref[...], preferred_element_type=F32) + b_ref[...]
    if emit_kv:
        @pl.when((j == COL_K) | (j == COL_V))
        def _():
            kv_ref[...] = acc
    sig = jax.nn.sigmoid(acc)
    silu_col = (j == COL_GA) | (j == COL_GC)
    out = jnp.where(j == COL_UG, sig, jnp.where(silu_col, acc * sig, acc))
    z_ref[...] = out.astype(BF16)


def _inproj(x, norm_g, mod, w_in, b_in, *, rows_per_mod, mod_row0, emit_kv):
    t = x.shape[0]
    tm, tn = 1024, 1024
    tiles_per_mod = rows_per_mod // tm

    def mod_idx(part):
        return lambda i, j: ((mod_row0 + i // tiles_per_mod) * 3 + part, 0, 0)

    out_shape = [jax.ShapeDtypeStruct((t, D_IN), BF16)]
    out_specs = [pl.BlockSpec((tm, tn), lambda i, j: (i, j))]
    if emit_kv:
        out_shape.append(jax.ShapeDtypeStruct((t, 2 * D_ATTN), F32))
        out_specs.append(pl.BlockSpec((tm, tn), lambda i, j: (i, jnp.clip(j - COL_K, 0, 1))))
    return pl.pallas_call(
        functools.partial(_inproj_kernel, emit_kv=emit_kv),
        grid=(t // tm, D_IN // tn),
        in_specs=[
            pl.BlockSpec((tm, D_MODEL), lambda i, j: (i, 0)),
            pl.BlockSpec((1, D_MODEL), lambda i, j: (0, 0)),
            pl.BlockSpec((None, 1, D_MODEL), mod_idx(0)),
            pl.BlockSpec((None, 1, D_MODEL), mod_idx(1)),
            pl.BlockSpec((D_MODEL, tn), lambda i, j: (0, j)),
            pl.BlockSpec((1, tn), lambda i, j: (0, j)),
        ],
        out_specs=out_specs,
        out_shape=out_shape,
        scratch_shapes=[pltpu.VMEM((tm, D_MODEL), BF16)],
        compiler_params=_params("arbitrary", "arbitrary"),
        name="inproj_kv" if emit_kv else "inproj",
    )(x, norm_g.reshape(1, D_MODEL), mod, mod, w_in, b_in.reshape(1, D_IN))


def _dot_nt(a, b):
    return lax.dot_general(a, b, (((1,), (1,)), ((), ())), preferred_element_type=F32)


def _ctx_attn_kernel(q_ref, k_ref, v_ref, ga_ref, o_ref):
    scale = HEAD_DIM ** -0.5
    for h in range(N_HEADS_A):
        cols = slice(h * HEAD_DIM, (h + 1) * HEAD_DIM)
        s = _dot_nt(q_ref[:, cols], k_ref[:, cols]) * scale
        p = jnp.exp(s - jnp.max(s, axis=-1, keepdims=True))
        denom = jnp.sum(p, axis=-1, keepdims=True)
        o = jnp.dot(p.astype(BF16), v_ref[:, cols], preferred_element_type=F32)
        o_ref[:, cols] = (o / denom * ga_ref[:, cols].astype(F32)).astype(BF16)


def _ctx_attention(z, seq):
    t = z.shape[0]

    def col(cb):
        return pl.BlockSpec((seq, D_ATTN), lambda b: (b, cb))

    return pl.pallas_call(
        _ctx_attn_kernel,
        grid=(t // seq,),
        in_specs=[col(COL_Q), col(COL_K), col(COL_V), col(COL_GA)],
        out_specs=pl.BlockSpec((seq, D_ATTN), lambda b: (b, 0)),
        out_shape=jax.ShapeDtypeStruct((t, D_ATTN), BF16),
        compiler_params=_params("arbitrary"),
        name="ctx_attention",
    )(z, z, z, z)


def _nbr_bias_indices(rows):
    n_groups = rows // Q_ROWS
    a = np.arange(Q_ROWS)[:, None]
    tt = np.arange(K_ROWS)[None, :]
    j = np.arange(GRID_W)[:, None]
    cc = np.arange(GRID_W)[None, :]
    cs = np.clip(j - WIN_C // 2, 0, GRID_W - WIN_C)
    valid_c = (cc >= cs) & (cc < cs + WIN_C)
    dc = np.clip(cc - j + WIN_C - 1, 0, 2 * WIN_C - 2)
    drs, dcs, valids = [], [], []
    for g in (0, 1, n_groups - 1):
        base = np.clip(Q_ROWS * g - WIN_R // 2, 0, rows - K_ROWS)
        i = Q_ROWS * g + a
        rs = np.clip(i - WIN_R // 2, 0, rows - WIN_R)
        r = base + tt
        valid_r = (r >= rs) & (r < rs + WIN_R)
        dr = np.clip(r - i + WIN_R - 1, 0, 2 * WIN_R - 2)
        shape = (Q_ROWS, GRID_W, K_ROWS, GRID_W)
        dr_f = np.broadcast_to(dr[:, None, :, None], shape)
        dc_f = np.broadcast_to(dc[None, :, None, :], shape)
        v_f = valid_r[:, None, :, None] & valid_c[None, :, None, :]
        flat = (Q_ROWS * GRID_W, K_ROWS * GRID_W)
        drs.append(dr_f.reshape(flat))
        dcs.append(dc_f.reshape(flat))
        valids.append(v_f.reshape(flat))
    return np.stack(drs), np.stack(dcs), np.stack(valids)


def _nbr_bias_table(rpb_l, rows):
    dr, dc, valid = _nbr_bias_indices(rows)
    tbl = rpb_l[:, dr, dc]
    return jnp.where(valid[None], tbl, MASKED).astype(F32)


def _nbr_attn_kernel(q_ref, k_ref, v_ref, ga_ref, ck_ref, cv_ref, bias_ref, o_ref, *, rows):
    scale = HEAD_DIM ** -0.5
    n_groups = rows // Q_ROWS
    nq = Q_ROWS * GRID_W
    nk = K_ROWS * GRID_W
    kc = ck_ref[...].astype(BF16)
    vc = cv_ref[...].astype(BF16)

    def group(g, carry):
        base = jnp.clip(Q_ROWS * g - WIN_R // 2, 0, rows - K_ROWS)
        kind = jnp.where(g == 0, 0, jnp.where(g == n_groups - 1, 2, 1))
        q0 = pl.multiple_of(g * nq, nq)
        k0 = pl.multiple_of(base * GRID_W, GRID_W)
        q = q_ref[pl.ds(q0, nq), :]
        s_loc = _dot_nt(q, k_ref[pl.ds(k0, nk), :]) * scale + bias_ref[kind]
        s_ctx = _dot_nt(q, kc) * scale
        m = jnp.maximum(jnp.max(s_loc, axis=-1, keepdims=True),
                        jnp.max(s_ctx, axis=-1, keepdims=True))
        p_loc = jnp.exp(s_loc - m)
        p_ctx = jnp.exp(s_ctx - m)
        denom = jnp.sum(p_loc, axis=-1, keepdims=True) + jnp.sum(p_ctx, axis=-1, keepdims=True)
        o = (jnp.dot(p_loc.astype(BF16), v_ref[pl.ds(k0, nk), :], preferred_element_type=F32)
             + jnp.dot(p_ctx.astype(BF16), vc, preferred_element_type=F32))
        o_ref[pl.ds(q0, nq), :] = (o / denom * ga_ref[pl.ds(q0, nq), :].astype(F32)).astype(BF16)
        return carry

    lax.fori_loop(0, n_groups, group, 0)


def _nbr_attention(z, cache_k, cache_v, bias, layer, seq):
    t = z.shape[0]
    rows = seq // GRID_W
    past = cache_k.shape[2]

    def col(cb):
        return pl.BlockSpec((seq, HEAD_DIM), lambda h, b: (b, cb * N_HEADS_A + h))

    depth = cache_k.shape[1]
    cache_k = cache_k.reshape(t // seq, depth, past, D_ATTN)
    cache_v = cache_v.reshape(t // seq, depth, past, D_ATTN)
    cache_spec = pl.BlockSpec((None, None, past, HEAD_DIM), lambda h, b: (b, layer, 0, h))
    return pl.pallas_call(
        functools.partial(_nbr_attn_kernel, rows=rows),
        grid=(N_HEADS_A, t // seq),
        in_specs=[col(COL_Q), col(COL_K), col(COL_V), col(COL_GA), cache_spec, cache_spec,
                  pl.BlockSpec((None, 3, Q_ROWS * GRID_W, K_ROWS * GRID_W),
                               lambda h, b: (h, 0, 0, 0))],
        out_specs=pl.BlockSpec((seq, HEAD_DIM), lambda h, b: (b, h)),
        out_shape=jax.ShapeDtypeStruct((t, D_ATTN), BF16),
        compiler_params=_params("arbitrary", "arbitrary"),
        name="nbr_attention",
    )(z, z, z, z, cache_k, cache_v, bias)


CONV_HALO = 16
CONV_CHUNK = 64


def _conv_kernel(u_ref, up_ref, un_ref, s_ref, sp_ref, sn_ref, gc_ref, dww_ref, dwb_ref,
                 lng_ref, lnb_ref, w_ref, b_ref, o_ref, ext_ref, conv_ref, *, tm, tiles_per_seq):
    i = pl.program_id(0)
    first = (i % tiles_per_seq) == 0
    last = (i % tiles_per_seq) == tiles_per_seq - 1
    halo_tiles = CONV_HALO // SUBLANES
    n_tiles = tm // SUBLANES

    def glu(u, s):
        return u[...].astype(F32) * s[...].astype(F32)

    def tiles(a):
        return a.reshape(a.shape[0] // SUBLANES, SUBLANES, D_CONV)

    ext_ref[0:halo_tiles] = tiles(jnp.where(first, 0.0, glu(up_ref, sp_ref)))
    ext_ref[halo_tiles:halo_tiles + n_tiles] = tiles(glu(u_ref, s_ref))
    ext_ref[halo_tiles + n_tiles:] = tiles(jnp.where(last, 0.0, glu(un_ref, sn_ref)))

    chunk_tiles = CONV_CHUNK // SUBLANES
    win_tiles = chunk_tiles + 2 * halo_tiles
    off = CONV_HALO - CONV_K // 2

    def chunk(cidx, carry):
        c0 = cidx * chunk_tiles
        for cb in range(D_CONV // LANES):
            lanes = slice(cb * LANES, (cb + 1) * LANES)
            win = ext_ref[pl.ds(c0, win_tiles), :, lanes].reshape(win_tiles * SUBLANES, LANES)
            acc = jnp.zeros((CONV_CHUNK, LANES), F32)
            for s in range(SUBLANES):
                taps = [k for k in range(CONV_K) if (k + off) % SUBLANES == s]
                if not taps:
                    continue
                span = (max(taps) + off - s) + CONV_CHUNK
                sh = win[s:s + span]
                for k in taps:
                    a0 = k + off - s
                    acc = acc + sh[a0:a0 + CONV_CHUNK] * dww_ref[k:k + 1, lanes]
            conv_ref[pl.ds(c0, chunk_tiles), :, lanes] = acc.reshape(chunk_tiles, SUBLANES, LANES)
        return carry

    lax.fori_loop(0, tm // CONV_CHUNK, chunk, 0)

    a = conv_ref[...].reshape(tm, D_CONV) + dwb_ref[...]
    mu = jnp.mean(a, axis=-1, keepdims=True)
    d = a - mu
    var = jnp.mean(d * d, axis=-1, keepdims=True)
    y = d * lax.rsqrt(var + EPS) * lng_ref[...] + lnb_ref[...]
    y = y * jax.nn.sigmoid(y)
    out = jnp.dot(y.astype(BF16), w_ref[...], preferred_element_type=F32) + b_ref[...]
    o_ref[...] = (out * gc_ref[...].astype(F32)).astype(BF16)


def _conformer_conv(z, dw_w, dw_b, ln_g, ln_b, w_pw2, b_pw2, *, seq, tm):
    t = z.shape[0]
    tiles_per_seq = seq // tm
    hb = tm // CONV_HALO
    n_hb = t // CONV_HALO

    def main(cb):
        return pl.BlockSpec((tm, D_CONV), lambda i: (i, cb))

    def before(cb):
        return pl.BlockSpec((CONV_HALO, D_CONV), lambda i: (jnp.maximum(i * hb - 1, 0), cb))

    def after(cb):
        return pl.BlockSpec((CONV_HALO, D_CONV),
                            lambda i: (jnp.minimum((i + 1) * hb, n_hb - 1), cb))

    def full(shape):
        return pl.BlockSpec(shape, lambda i: (0,) * len(shape))

    ext_tiles = (tm + 2 * CONV_HALO) // SUBLANES
    return pl.pallas_call(
        functools.partial(_conv_kernel, tm=tm, tiles_per_seq=tiles_per_seq),
        grid=(t // tm,),
        in_specs=[main(COL_U), before(COL_U), after(COL_U),
                  main(COL_UG), before(COL_UG), after(COL_UG), main(COL_GC),
                  full((CONV_K, D_CONV)), full((1, D_CONV)), full((1, D_CONV)), full((1, D_CONV)),
                  full((D_CONV, D_CONV)), full((1, D_CONV))],
        out_specs=pl.BlockSpec((tm, D_CONV), lambda i: (i, 0)),
        out_shape=jax.ShapeDtypeStruct((t, D_CONV), BF16),
        scratch_shapes=[pltpu.VMEM((ext_tiles, SUBLANES, D_CONV), F32),
                        pltpu.VMEM((tm // SUBLANES, SUBLANES, D_CONV), F32)],
        compiler_params=_params("arbitrary"),
        name="conformer_conv",
    )(z, z, z, z, z, z, z, dw_w, dw_b.reshape(1, D_CONV), ln_g.reshape(1, D_CONV),
      ln_b.reshape(1, D_CONV), w_pw2, b_pw2.reshape(1, D_CONV))


def _outproj_kernel(x_ref, a_ref, c_ref, gate_ref, w_ref, fg_ref, o_ref, cat_ref, *, final):
    cat_ref[:, :D_ATTN] = a_ref[...]
    cat_ref[:, D_ATTN:] = c_ref[...]
    out = jnp.dot(cat_ref[...], w_ref[...], preferred_element_type=F32)
    x = x_ref[...] + gate_ref[...] * out
    if final:
        x = x * lax.rsqrt(jnp.mean(x * x, axis=-1, keepdims=True) + EPS) * fg_ref[...]
    o_ref[...] = x


def _outproj(x, attn, conv, mod, w_out, final_g, *, rows_per_mod, mod_row0, final):
    t = x.shape[0]
    tm = 512
    tiles_per_mod = rows_per_mod // tm
    return pl.pallas_call(
        functools.partial(_outproj_kernel, final=final),
        grid=(t // tm,),
        in_specs=[
            pl.BlockSpec((tm, D_MODEL), lambda i: (i, 0)),
            pl.BlockSpec((tm, D_ATTN), lambda i: (i, 0)),
            pl.BlockSpec((tm, D_CONV), lambda i: (i, 0)),
            pl.BlockSpec((None, 1, D_MODEL),
                         lambda i: ((mod_row0 + i // tiles_per_mod) * 3 + 2, 0, 0)),
            pl.BlockSpec((D_MODEL, D_MODEL), lambda i: (0, 0)),
            pl.BlockSpec((1, D_MODEL), lambda i: (0, 0)),
        ],
        out_specs=pl.BlockSpec((tm, D_MODEL), lambda i: (i, 0)),
        out_shape=jax.ShapeDtypeStruct((t, D_MODEL), F32),
        scratch_shapes=[pltpu.VMEM((tm, D_MODEL), BF16)],
        compiler_params=_params("arbitrary"),
        name="outproj_final" if final else "outproj",
    )(x, attn, conv, mod, w_out, final_g.reshape(1, D_MODEL))


def kernel(x_prompt, x_sample, cache_k, cache_v, c, c_ctx, norm_g, w_ada, b_ada, w_in, b_in, rpb,
           dw_w, dw_b, cln_g, cln_b, w_pw2, b_pw2, w_out, final_norm_g):
    batch, seq, _ = x_prompt.shape
    dec_batch, dec_seq, _ = x_sample.shape
    assert dec_batch + 1 <= MOD_ROWS

    cvec = jnp.zeros((MOD_ROWS, D_MODEL), F32).at[:dec_batch].set(c).at[dec_batch].set(c_ctx)
    mod = _modulation(cvec, w_ada, b_ada).reshape(DEPTH, MOD_ROWS * 3, 1, D_MODEL)

    w_in_b = w_in.astype(BF16)
    w_pw2_b = w_pw2.astype(BF16)
    w_out_b = w_out.astype(BF16)

    xp = x_prompt.reshape(batch * seq, D_MODEL)
    xs = x_sample.reshape(dec_batch * dec_seq, D_MODEL)
    k_list, v_list = [], []
    for l in range(DEPTH):
        final = l == DEPTH - 1
        conv_w = (dw_w[l], dw_b[l], cln_g[l], cln_b[l], w_pw2_b[l], b_pw2[l])
        prompt_mod = dict(rows_per_mod=batch * seq, mod_row0=dec_batch)
        sample_mod = dict(rows_per_mod=dec_seq, mod_row0=0)

        zp, kvp = _inproj(xp, norm_g[l], mod[l], w_in_b[l], b_in[l], emit_kv=True, **prompt_mod)
        zs, = _inproj(xs, norm_g[l], mod[l], w_in_b[l], b_in[l], emit_kv=False, **sample_mod)
        k_list.append(kvp[:, :D_ATTN].reshape(batch, seq, N_HEADS_A, HEAD_DIM))
        v_list.append(kvp[:, D_ATTN:].reshape(batch, seq, N_HEADS_A, HEAD_DIM))

        ap = _ctx_attention(zp, seq)
        bias = _nbr_bias_table(rpb[l], dec_seq // GRID_W)
        a_s = _nbr_attention(zs, cache_k, cache_v, bias, l, dec_seq)

        cp = _conformer_conv(zp, *conv_w, seq=seq, tm=seq)
        cs = _conformer_conv(zs, *conv_w, seq=dec_seq, tm=512)

        xp = _outproj(xp, ap, cp, mod[l], w_out_b[l], final_norm_g, final=final, **prompt_mod)
        xs = _outproj(xs, a_s, cs, mod[l], w_out_b[l], final_norm_g, final=final, **sample_mod)

    y_prompt = xp.reshape(batch, seq, D_MODEL)
    y_sample = xs.reshape(dec_batch, dec_seq, D_MODEL)
    return (y_prompt, y_sample, jnp.stack(k_list, axis=1), jnp.stack(v_list, axis=1))
```

```python
import functools

import numpy as np
import jax
import jax.numpy as jnp
from jax import lax
from jax.experimental import pallas as pl
from jax.experimental.pallas import tpu as pltpu

D_MODEL = 2048
DEPTH = 2
GRID_W = 64
N_HEADS_A = 8
HEAD_DIM = 128
D_ATTN = N_HEADS_A * HEAD_DIM
D_CONV = D_MODEL - D_ATTN
D_IN = 4 * D_ATTN + 3 * D_CONV
WIN_R = 8
WIN_C = 16
CONV_K = 31
EPS = 1e-6

F32 = jnp.float32
BF16 = jnp.bfloat16

VMEM_LIMIT_BYTES = 56 * 1024 * 1024
LANES = 128
SUBLANES = 8
BF16_ROWS = 16

MOD_ROWS = 16
MASKED = -1e30
LOG2E = float(np.log2(np.e))
Q_SCALE = HEAD_DIM ** -0.5 * LOG2E

COL_Q, COL_K, COL_V, COL_GA, COL_U, COL_UG, COL_GC = range(7)

Q_ROWS = 4
K_ROWS = 12


def _params(*sem):
    return pltpu.CompilerParams(dimension_semantics=sem, vmem_limit_bytes=VMEM_LIMIT_BYTES)


def _mod_kernel(c_ref, w_ref, b_ref, o_ref):
    c = c_ref[...]
    s = c * jax.nn.sigmoid(c)
    o_ref[...] = jnp.dot(s.astype(BF16), w_ref[...].astype(BF16),
                         preferred_element_type=F32) + b_ref[...]


def _modulation(cvec, w_ada, b_ada):
    tn = 1536
    return pl.pallas_call(
        _mod_kernel,
        grid=(DEPTH, 3 * D_MODEL // tn),
        in_specs=[
            pl.BlockSpec((MOD_ROWS, D_MODEL), lambda l, j: (0, 0)),
            pl.BlockSpec((None, D_MODEL, tn), lambda l, j: (l, 0, j)),
            pl.BlockSpec((None, 1, tn), lambda l, j: (l, 0, j)),
        ],
        out_specs=pl.BlockSpec((None, MOD_ROWS, tn), lambda l, j: (l, 0, j)),
        out_shape=jax.ShapeDtypeStruct((DEPTH, MOD_ROWS, 3 * D_MODEL), F32),
        compiler_params=_params("arbitrary", "arbitrary"),
        name="modulation",
    )(cvec, w_ada, b_ada.reshape(DEPTH, 1, 3 * D_MODEL))


def _inproj_kernel(x_ref, g_ref, shift_ref, scl_ref, w_ref, b_ref, *rest, emit_kv):
    if emit_kv:
        z_ref, kv_ref, h_ref = rest
    else:
        z_ref, h_ref = rest
    j = pl.program_id(1)

    @pl.when(j == 0)
    def _():
        x = x_ref[...]
        y = x * lax.rsqrt(jnp.mean(x * x, axis=-1, keepdims=True) + EPS) * g_ref[...]
        h_ref[...] = (y * (1.0 + scl_ref[...]) + shift_ref[...]).astype(BF16)

    acc = jnp.dot(h_ref[...], w_ref[...], preferred_element_type=F32) + b_ref[...]
    if emit_kv:
        @pl.when((j == COL_K) | (j == COL_V))
        def _():
            kv_ref[...] = acc
    sig = jax.nn.sigmoid(acc)
    silu_col = (j == COL_GA) | (j == COL_GC)
    out = jnp.where(j == COL_UG, sig, jnp.where(silu_col, acc * sig, acc))
    out = jnp.where(j == COL_Q, acc * Q_SCALE, out)
    z_ref[...] = out.astype(BF16)


def _inproj(x, norm_g, mod, w_in, b_in, *, rows_per_mod, mod_row0, emit_kv):
    t = x.shape[0]
    tm, tn = 1024, 1024
    tiles_per_mod = rows_per_mod // tm

    def mod_idx(part):
        return lambda i, j: ((mod_row0 + i // tiles_per_mod) * 3 + part, 0, 0)

    out_shape = [jax.ShapeDtypeStruct((t, D_IN), BF16)]
    out_specs = [pl.BlockSpec((tm, tn), lambda i, j: (i, j))]
    if emit_kv:
        out_shape.append(jax.ShapeDtypeStruct((t, 2 * D_ATTN), F32))
        out_specs.append(pl.BlockSpec((tm, tn), lambda i, j: (i, jnp.clip(j - COL_K, 0, 1))))
    return pl.pallas_call(
        functools.partial(_inproj_kernel, emit_kv=emit_kv),
        grid=(t // tm, D_IN // tn),
        in_specs=[
            pl.BlockSpec((tm, D_MODEL), lambda i, j: (i, 0)),
            pl.BlockSpec((1, D_MODEL), lambda i, j: (0, 0)),
            pl.BlockSpec((None, 1, D_MODEL), mod_idx(0)),
            pl.BlockSpec((None, 1, D_MODEL), mod_idx(1)),
            pl.BlockSpec((D_MODEL, tn), lambda i, j: (0, j)),
            pl.BlockSpec((1, tn), lambda i, j: (0, j)),
        ],
        out_specs=out_specs,
        out_shape=out_shape,
        scratch_shapes=[pltpu.VMEM((tm, D_MODEL), BF16)],
        compiler_params=_params("arbitrary", "arbitrary"),
        name="inproj_kv" if emit_kv else "inproj",
    )(x, norm_g.reshape(1, D_MODEL), mod, mod, w_in, b_in.reshape(1, D_IN))


def _dot_nt(a, b):
    return lax.dot_general(a, b, (((1,), (1,)), ((), ())), preferred_element_type=F32)


def _ctx_attn_kernel(q_ref, k_ref, v_ref, ga_ref, o_ref):
    for h in range(N_HEADS_A):
        cols = slice(h * HEAD_DIM, (h + 1) * HEAD_DIM)
        s = _dot_nt(q_ref[:, cols], k_ref[:, cols])
        p = jnp.exp2(s - jnp.max(s, axis=-1, keepdims=True))
        denom = jnp.sum(p, axis=-1, keepdims=True)
        o = jnp.dot(p.astype(BF16), v_ref[:, cols], preferred_element_type=F32)
        o_ref[:, cols] = (o / denom * ga_ref[:, cols].astype(F32)).astype(BF16)


def _ctx_attention(z, seq):
    t = z.shape[0]

    def col(cb):
        return pl.BlockSpec((seq, D_ATTN), lambda b: (b, cb))

    return pl.pallas_call(
        _ctx_attn_kernel,
        grid=(t // seq,),
        in_specs=[col(COL_Q), col(COL_K), col(COL_V), col(COL_GA)],
        out_specs=pl.BlockSpec((seq, D_ATTN), lambda b: (b, 0)),
        out_shape=jax.ShapeDtypeStruct((t, D_ATTN), BF16),
        compiler_params=_params("arbitrary"),
        name="ctx_attention",
    )(z, z, z, z)


def _nbr_bias_table(rpb_l, rows):
    n_groups = rows // Q_ROWS
    j = np.arange(GRID_W)[:, None]
    cc = np.arange(GRID_W)[None, :]
    cs = np.clip(j - WIN_C // 2, 0, GRID_W - WIN_C)
    valid_c = (cc >= cs) & (cc < cs + WIN_C)
    dc = cc - j + WIN_C - 1
    onehot = ((dc[None] == np.arange(2 * WIN_C - 1)[:, None, None]) & valid_c[None])
    toe = jnp.einsum('hrd,djc->hrjc', rpb_l, onehot.astype(np.float32),
                     precision=lax.Precision.HIGHEST)
    toe = jnp.where(valid_c, toe * LOG2E, MASKED).astype(F32)
    masked = jnp.full((N_HEADS_A, GRID_W, GRID_W), MASKED, F32)
    kinds = []
    for g in (0, 1, n_groups - 1):
        base = min(max(Q_ROWS * g - WIN_R // 2, 0), rows - K_ROWS)
        q_rows = []
        for a in range(Q_ROWS):
            i = Q_ROWS * g + a
            rs = min(max(i - WIN_R // 2, 0), rows - WIN_R)
            blocks = [toe[:, base + t - i + WIN_R - 1] if rs <= base + t < rs + WIN_R else masked
                      for t in range(K_ROWS)]
            q_rows.append(jnp.concatenate(blocks, axis=-1))
        kinds.append(jnp.concatenate(q_rows, axis=-2))
    return jnp.stack(kinds, axis=1)


def _nbr_attn_kernel(q_ref, k_ref, v_ref, ga_ref, ck_ref, cv_ref, bias_ref, o_ref, *, rows):
    n_groups = rows // Q_ROWS
    nq = Q_ROWS * GRID_W
    nk = K_ROWS * GRID_W
    kc = ck_ref[...].astype(BF16)
    vc = cv_ref[...].astype(BF16)

    def group(g, carry):
        base = jnp.clip(Q_ROWS * g - WIN_R // 2, 0, rows - K_ROWS)
        kind = jnp.where(g == 0, 0, jnp.where(g == n_groups - 1, 2, 1))
        q0 = pl.multiple_of(g * nq, nq)
        k0 = pl.multiple_of(base * GRID_W, GRID_W)
        q = q_ref[pl.ds(q0, nq), :]
        s_loc = _dot_nt(q, k_ref[pl.ds(k0, nk), :]) + bias_ref[kind]
        s_ctx = _dot_nt(q, kc)
        m = jnp.maximum(jnp.max(s_loc, axis=-1, keepdims=True),
                        jnp.max(s_ctx, axis=-1, keepdims=True))
        p_loc = jnp.exp2(s_loc - m)
        p_ctx = jnp.exp2(s_ctx - m)
        denom = jnp.sum(p_loc, axis=-1, keepdims=True) + jnp.sum(p_ctx, axis=-1, keepdims=True)
        o = (jnp.dot(p_loc.astype(BF16), v_ref[pl.ds(k0, nk), :], preferred_element_type=F32)
             + jnp.dot(p_ctx.astype(BF16), vc, preferred_element_type=F32))
        o_ref[pl.ds(q0, nq), :] = (o / denom * ga_ref[pl.ds(q0, nq), :].astype(F32)).astype(BF16)
        return carry

    lax.fori_loop(0, n_groups, group, 0, unroll=2)


def _nbr_attention(z, cache_k, cache_v, bias, layer, seq):
    t = z.shape[0]
    rows = seq // GRID_W
    past = cache_k.shape[2]

    def col(cb):
        return pl.BlockSpec((seq, HEAD_DIM), lambda h, b: (b, cb * N_HEADS_A + h))

    depth = cache_k.shape[1]
    cache_k = cache_k.reshape(t // seq, depth, past, D_ATTN)
    cache_v = cache_v.reshape(t // seq, depth, past, D_ATTN)
    cache_spec = pl.BlockSpec((None, None, past, HEAD_DIM), lambda h, b: (b, layer, 0, h))
    return pl.pallas_call(
        functools.partial(_nbr_attn_kernel, rows=rows),
        grid=(N_HEADS_A, t // seq),
        in_specs=[col(COL_Q), col(COL_K), col(COL_V), col(COL_GA), cache_spec, cache_spec,
                  pl.BlockSpec((None, 3, Q_ROWS * GRID_W, K_ROWS * GRID_W),
                               lambda h, b: (h, 0, 0, 0))],
        out_specs=pl.BlockSpec((seq, HEAD_DIM), lambda h, b: (b, h)),
        out_shape=jax.ShapeDtypeStruct((t, D_ATTN), BF16),
        compiler_params=_params("arbitrary", "arbitrary"),
        name="nbr_attention",
    )(z, z, z, z, cache_k, cache_v, bias)


CONV_HALO = 16
CONV_CHUNK = 64


def _conv_kernel(u_ref, up_ref, un_ref, s_ref, sp_ref, sn_ref, gc_ref, dww_ref, dwb_ref,
                 lng_ref, lnb_ref, w_ref, b_ref, o_ref, ext_ref, conv_ref, *, tm, tiles_per_seq):
    i = pl.program_id(0)
    first = (i % tiles_per_seq) == 0
    last = (i % tiles_per_seq) == tiles_per_seq - 1
    halo_tiles = CONV_HALO // SUBLANES
    n_tiles = tm // SUBLANES

    def glu(u, s):
        return u[...].astype(F32) * s[...].astype(F32)

    def tiles(a):
        return a.reshape(a.shape[0] // SUBLANES, SUBLANES, D_CONV)

    ext_ref[0:halo_tiles] = tiles(jnp.where(first, 0.0, glu(up_ref, sp_ref)))
    ext_ref[halo_tiles:halo_tiles + n_tiles] = tiles(glu(u_ref, s_ref))
    ext_ref[halo_tiles + n_tiles:] = tiles(jnp.where(last, 0.0, glu(un_ref, sn_ref)))

    chunk_tiles = CONV_CHUNK // SUBLANES
    win_tiles = chunk_tiles + 2 * halo_tiles
    off = CONV_HALO - CONV_K // 2

    def chunk(cidx, carry):
        c0 = cidx * chunk_tiles
        for cb in range(D_CONV // LANES):
            lanes = slice(cb * LANES, (cb + 1) * LANES)
            win = ext_ref[pl.ds(c0, win_tiles), :, lanes].reshape(win_tiles * SUBLANES, LANES)
            acc = jnp.zeros((CONV_CHUNK, LANES), F32)
            for s in range(SUBLANES):
                taps = [k for k in range(CONV_K) if (k + off) % SUBLANES == s]
                if not taps:
                    continue
                sh = win if s == 0 else pltpu.roll(win, win_tiles * SUBLANES - s, axis=0)
                for k in taps:
                    a0 = k + off - s
                    acc = acc + sh[a0:a0 + CONV_CHUNK] * dww_ref[k:k + 1, lanes]
            conv_ref[pl.ds(c0, chunk_tiles), :, lanes] = acc.reshape(chunk_tiles, SUBLANES, LANES)
        return carry

    lax.fori_loop(0, tm // CONV_CHUNK, chunk, 0)

    a = conv_ref[...].reshape(tm, D_CONV) + dwb_ref[...]
    mu = jnp.mean(a, axis=-1, keepdims=True)
    d = a - mu
    var = jnp.mean(d * d, axis=-1, keepdims=True)
    y = d * lax.rsqrt(var + EPS) * lng_ref[...] + lnb_ref[...]
    y = y * jax.nn.sigmoid(y)
    out = jnp.dot(y.astype(BF16), w_ref[...], preferred_element_type=F32) + b_ref[...]
    o_ref[...] = (out * gc_ref[...].astype(F32)).astype(BF16)


def _conformer_conv(z, dw_w, dw_b, ln_g, ln_b, w_pw2, b_pw2, *, seq, tm):
    t = z.shape[0]
    tiles_per_seq = seq // tm
    hb = tm // CONV_HALO
    n_hb = t // CONV_HALO

    def main(cb):
        return pl.BlockSpec((tm, D_CONV), lambda i: (i, cb))

    def before(cb):
        return pl.BlockSpec((CONV_HALO, D_CONV), lambda i: (jnp.maximum(i * hb - 1, 0), cb))

    def after(cb):
        return pl.BlockSpec((CONV_HALO, D_CONV),
                            lambda i: (jnp.minimum((i + 1) * hb, n_hb - 1), cb))

    def full(shape):
        return pl.BlockSpec(shape, lambda i: (0,) * len(shape))

    ext_tiles = (tm + 2 * CONV_HALO) // SUBLANES
    return pl.pallas_call(
        functools.partial(_conv_kernel, tm=tm, tiles_per_seq=tiles_per_seq),
        grid=(t // tm,),
        in_specs=[main(COL_U), before(COL_U), after(COL_U),
                  main(COL_UG), before(COL_UG), after(COL_UG), main(COL_GC),
                  full((CONV_K, D_CONV)), full((1, D_CONV)), full((1, D_CONV)), full((1, D_CONV)),
                  full((D_CONV, D_CONV)), full((1, D_CONV))],
        out_specs=pl.BlockSpec((tm, D_CONV), lambda i: (i, 0)),
        out_shape=jax.ShapeDtypeStruct((t, D_CONV), BF16),
        scratch_shapes=[pltpu.VMEM((ext_tiles, SUBLANES, D_CONV), F32),
                        pltpu.VMEM((tm // SUBLANES, SUBLANES, D_CONV), F32)],
        compiler_params=_params("arbitrary"),
        name="conformer_conv",
    )(z, z, z, z, z, z, z, dw_w, dw_b.reshape(1, D_CONV), ln_g.reshape(1, D_CONV),
      ln_b.reshape(1, D_CONV), w_pw2, b_pw2.reshape(1, D_CONV))


def _outproj_kernel(x_ref, a_ref, c_ref, gate_ref, w_ref, fg_ref, o_ref, cat_ref, *, final):
    cat_ref[:, :D_ATTN] = a_ref[...]
    cat_ref[:, D_ATTN:] = c_ref[...]
    out = jnp.dot(cat_ref[...], w_ref[...], preferred_element_type=F32)
    x = x_ref[...] + gate_ref[...] * out
    if final:
        x = x * lax.rsqrt(jnp.mean(x * x, axis=-1, keepdims=True) + EPS) * fg_ref[...]
    o_ref[...] = x


def _outproj(x, attn, conv, mod, w_out, final_g, *, rows_per_mod, mod_row0, final):
    t = x.shape[0]
    tm = 512
    tiles_per_mod = rows_per_mod // tm
    return pl.pallas_call(
        functools.partial(_outproj_kernel, final=final),
        grid=(t // tm,),
        in_specs=[
            pl.BlockSpec((tm, D_MODEL), lambda i: (i, 0)),
            pl.BlockSpec((tm, D_ATTN), lambda i: (i, 0)),
            pl.BlockSpec((tm, D_CONV), lambda i: (i, 0)),
            pl.BlockSpec((None, 1, D_MODEL),
                         lambda i: ((mod_row0 + i // tiles_per_mod) * 3 + 2, 0, 0)),
            pl.BlockSpec((D_MODEL, D_MODEL), lambda i: (0, 0)),
            pl.BlockSpec((1, D_MODEL), lambda i: (0, 0)),
        ],
        out_specs=pl.BlockSpec((tm, D_MODEL), lambda i: (i, 0)),
        out_shape=jax.ShapeDtypeStruct((t, D_MODEL), F32),
        scratch_shapes=[pltpu.VMEM((tm, D_MODEL), BF16)],
        compiler_params=_params("arbitrary"),
        name="outproj_final" if final else "outproj",
    )(x, attn, conv, mod, w_out, final_g.reshape(1, D_MODEL))


def kernel(x_prompt, x_sample, cache_k, cache_v, c, c_ctx, norm_g, w_ada, b_ada, w_in, b_in, rpb,
           dw_w, dw_b, cln_g, cln_b, w_pw2, b_pw2, w_out, final_norm_g):
    batch, seq, _ = x_prompt.shape
    dec_batch, dec_seq, _ = x_sample.shape
    assert dec_batch + 1 <= MOD_ROWS

    cvec = jnp.zeros((MOD_ROWS, D_MODEL), F32).at[:dec_batch].set(c).at[dec_batch].set(c_ctx)
    mod = _modulation(cvec, w_ada, b_ada).reshape(DEPTH, MOD_ROWS * 3, 1, D_MODEL)

    w_in_b = w_in.astype(BF16)
    w_pw2_b = w_pw2.astype(BF16)
    w_out_b = w_out.astype(BF16)

    xp = x_prompt.reshape(batch * seq, D_MODEL)
    xs = x_sample.reshape(dec_batch * dec_seq, D_MODEL)
    k_list, v_list = [], []
    for l in range(DEPTH):
        final = l == DEPTH - 1
        conv_w = (dw_w[l], dw_b[l], cln_g[l], cln_b[l], w_pw2_b[l], b_pw2[l])
        prompt_mod = dict(rows_per_mod=batch * seq, mod_row0=dec_batch)
        sample_mod = dict(rows_per_mod=dec_seq, mod_row0=0)

        zp, kvp = _inproj(xp, norm_g[l], mod[l], w_in_b[l], b_in[l], emit_kv=True, **prompt_mod)
        zs, = _inproj(xs, norm_g[l], mod[l], w_in_b[l], b_in[l], emit_kv=False, **sample_mod)
        k_list.append(kvp[:, :D_ATTN].reshape(batch, seq, N_HEADS_A, HEAD_DIM))
        v_list.append(kvp[:, D_ATTN:].reshape(batch, seq, N_HEADS_A, HEAD_DIM))

        ap = _ctx_attention(zp, seq)
        bias = _nbr_bias_table(rpb[l], dec_seq // GRID_W)
        a_s = _nbr_attention(zs, cache_k, cache_v, bias, l, dec_seq)

        cp = _conformer_conv(zp, *conv_w, seq=seq, tm=seq)
        cs = _conformer_conv(zs, *conv_w, seq=dec_seq, tm=512)

        xp = _outproj(xp, ap, cp, mod[l], w_out_b[l], final_norm_g, final=final, **prompt_mod)
        xs = _outproj(xs, a_s, cs, mod[l], w_out_b[l], final_norm_g, final=final, **sample_mod)

    y_prompt = xp.reshape(batch, seq, D_MODEL)
    y_sample = xs.reshape(dec_batch, dec_seq, D_MODEL)
    return (y_prompt, y_sample, jnp.stack(k_list, axis=1), jnp.stack(v_list, axis=1))
```

```python
import functools

import numpy as np
import jax
import jax.numpy as jnp
from jax import lax
from jax.experimental import pallas as pl
from jax.experimental.pallas import tpu as pltpu

D_MODEL = 2048
DEPTH = 2
GRID_W = 64
N_HEADS_A = 8
HEAD_DIM = 128
D_ATTN = N_HEADS_A * HEAD_DIM
D_CONV = D_MODEL - D_ATTN
D_IN = 4 * D_ATTN + 3 * D_CONV
WIN_R = 8
WIN_C = 16
CONV_K = 31
EPS = 1e-6

F32 = jnp.float32
BF16 = jnp.bfloat16

VMEM_LIMIT_BYTES = 56 * 1024 * 1024
LANES = 128
SUBLANES = 8

MOD_ROWS = 16
MASKED = -1e30
LOG2E = float(np.log2(np.e))
Q_SCALE = HEAD_DIM ** -0.5 * LOG2E

COL_Q, COL_K, COL_V, COL_GA, COL_U, COL_UG, COL_GC = range(7)

Q_ROWS = 4
K_ROWS = 12


def _params(*sem):
    return pltpu.CompilerParams(dimension_semantics=sem, vmem_limit_bytes=VMEM_LIMIT_BYTES)


def _mod_kernel(c_ref, w_ref, b_ref, o_ref):
    c = c_ref[...]
    s = c * jax.nn.sigmoid(c)
    o_ref[...] = jnp.dot(s.astype(BF16), w_ref[...].astype(BF16),
                         preferred_element_type=F32) + b_ref[...]


def _modulation(cvec, w_ada, b_ada):
    tn = 1536
    return pl.pallas_call(
        _mod_kernel,
        grid=(DEPTH, 3 * D_MODEL // tn),
        in_specs=[
            pl.BlockSpec((MOD_ROWS, D_MODEL), lambda l, j: (0, 0)),
            pl.BlockSpec((None, D_MODEL, tn), lambda l, j: (l, 0, j)),
            pl.BlockSpec((None, 1, tn), lambda l, j: (l, 0, j)),
        ],
        out_specs=pl.BlockSpec((None, MOD_ROWS, tn), lambda l, j: (l, 0, j)),
        out_shape=jax.ShapeDtypeStruct((DEPTH, MOD_ROWS, 3 * D_MODEL), F32),
        compiler_params=_params("arbitrary", "arbitrary"),
        name="modulation",
    )(cvec, w_ada, b_ada.reshape(DEPTH, 1, 3 * D_MODEL))


INPROJ_SUB = 256


def _inproj_kernel(x_ref, g_ref, shift_ref, scl_ref, w_ref, b_ref, *rest, emit_kv, aliased):
    rest = rest[2:] if aliased else rest
    if emit_kv:
        z_ref, kc_ref, vc_ref, h_ref = rest
    else:
        z_ref, h_ref = rest
    j = pl.program_id(1)

    @pl.when(j == 0)
    def _():
        x = x_ref[...]
        y = x * lax.rsqrt(jnp.mean(x * x, axis=-1, keepdims=True) + EPS) * g_ref[...]
        h_ref[...] = (y * (1.0 + scl_ref[...]) + shift_ref[...]).astype(BF16)

    silu_col = (j == COL_GA) | (j == COL_GC)
    plain_scale = jnp.where(j == COL_Q, Q_SCALE, 1.0)
    for m in range(x_ref.shape[0] // INPROJ_SUB):
        rows = pl.ds(m * INPROJ_SUB, INPROJ_SUB)
        acc = jnp.dot(h_ref[rows, :], w_ref[...], preferred_element_type=F32) + b_ref[...]
        if emit_kv:
            seq = kc_ref.shape[1]
            b_loc, r0 = divmod(m * INPROJ_SUB, seq)

            @pl.when(j == COL_K)
            def _():
                kc_ref[b_loc, r0:r0 + INPROJ_SUB, :] = acc

            @pl.when(j == COL_V)
            def _():
                vc_ref[b_loc, r0:r0 + INPROJ_SUB, :] = acc
        sig = jax.nn.sigmoid(acc)
        out = jnp.where(j == COL_UG, sig, acc * jnp.where(silu_col, sig, plain_scale))
        z_ref[rows, :] = out.astype(BF16)


def _inproj(x, norm_g, mod, w_in, b_in, *, rows_per_mod, mod_row0, kv=None):
    t = x.shape[0]
    tm, tn = (512 if kv is not None else 1024), 1024
    tiles_per_mod = rows_per_mod // tm

    def mod_idx(part):
        return lambda i, j: ((mod_row0 + i // tiles_per_mod) * 3 + part, 0, 0)

    args = [x, norm_g.reshape(1, D_MODEL), mod, mod, w_in, b_in.reshape(1, D_IN)]
    in_specs = [
        pl.BlockSpec((tm, D_MODEL), lambda i, j: (i, 0)),
        pl.BlockSpec((1, D_MODEL), lambda i, j: (0, 0)),
        pl.BlockSpec((None, 1, D_MODEL), mod_idx(0)),
        pl.BlockSpec((None, 1, D_MODEL), mod_idx(1)),
        pl.BlockSpec((D_MODEL, tn), lambda i, j: (0, j)),
        pl.BlockSpec((1, tn), lambda i, j: (0, j)),
    ]
    out_shape = [jax.ShapeDtypeStruct((t, D_IN), BF16)]
    out_specs = [pl.BlockSpec((tm, tn), lambda i, j: (i, j))]
    aliases = {}
    if kv is not None:
        layer, seq, caches = kv
        assert tm % seq == 0 and seq % INPROJ_SUB == 0
        cache_shape = jax.ShapeDtypeStruct((t // seq, DEPTH, seq, D_ATTN), F32)
        cache_spec = pl.BlockSpec((tm // seq, None, seq, D_ATTN), lambda i, j: (i, layer, 0, 0))
        out_shape += [cache_shape, cache_shape]
        out_specs += [cache_spec, cache_spec]
        if caches is not None:
            aliases = {len(args): 1, len(args) + 1: 2}
            args += list(caches)
            in_specs += [pl.BlockSpec(memory_space=pl.ANY)] * 2
    return pl.pallas_call(
        functools.partial(_inproj_kernel, emit_kv=kv is not None, aliased=bool(aliases)),
        grid=(t // tm, D_IN // tn),
        in_specs=in_specs,
        out_specs=out_specs,
        out_shape=out_shape,
        input_output_aliases=aliases,
        scratch_shapes=[pltpu.VMEM((tm, D_MODEL), BF16)],
        compiler_params=_params("arbitrary", "arbitrary"),
        name="inproj_kv" if kv is not None else "inproj",
    )(*args)


def _dot_nt(a, b):
    return lax.dot_general(a, b, (((1,), (1,)), ((), ())), preferred_element_type=F32)


def _ctx_attn_kernel(q_ref, k_ref, v_ref, ga_ref, o_ref):
    for h in range(N_HEADS_A):
        cols = slice(h * HEAD_DIM, (h + 1) * HEAD_DIM)
        s = _dot_nt(q_ref[:, cols], k_ref[:, cols])
        p = jnp.exp2(s - jnp.max(s, axis=-1, keepdims=True))
        denom = jnp.sum(p, axis=-1, keepdims=True)
        o = jnp.dot(p.astype(BF16), v_ref[:, cols], preferred_element_type=F32)
        o_ref[:, cols] = (o / denom * ga_ref[:, cols].astype(F32)).astype(BF16)


def _ctx_attention(z, seq):
    t = z.shape[0]

    def col(cb):
        return pl.BlockSpec((seq, D_ATTN), lambda b: (b, cb))

    return pl.pallas_call(
        _ctx_attn_kernel,
        grid=(t // seq,),
        in_specs=[col(COL_Q), col(COL_K), col(COL_V), col(COL_GA)],
        out_specs=pl.BlockSpec((seq, D_ATTN), lambda b: (b, 0)),
        out_shape=jax.ShapeDtypeStruct((t, D_ATTN), BF16),
        compiler_params=_params("arbitrary"),
        name="ctx_attention",
    )(z, z, z, z)


def _nbr_bias_table(rpb_l, rows):
    n_groups = rows // Q_ROWS
    j = np.arange(GRID_W)[:, None]
    cc = np.arange(GRID_W)[None, :]
    cs = np.clip(j - WIN_C // 2, 0, GRID_W - WIN_C)
    valid_c = (cc >= cs) & (cc < cs + WIN_C)
    dc = cc - j + WIN_C - 1
    onehot = ((dc[None] == np.arange(2 * WIN_C - 1)[:, None, None]) & valid_c[None])
    toe = jnp.einsum('hrd,djc->hrjc', rpb_l, onehot.astype(np.float32),
                     precision=lax.Precision.HIGHEST)
    toe = jnp.where(valid_c, toe * LOG2E, MASKED).astype(F32)
    masked = jnp.full((N_HEADS_A, GRID_W, GRID_W), MASKED, F32)
    kinds = []
    for g in (0, 1, n_groups - 1):
        base = min(max(Q_ROWS * g - WIN_R // 2, 0), rows - K_ROWS)
        q_rows = []
        for a in range(Q_ROWS):
            i = Q_ROWS * g + a
            rs = min(max(i - WIN_R // 2, 0), rows - WIN_R)
            blocks = [toe[:, base + t - i + WIN_R - 1] if rs <= base + t < rs + WIN_R else masked
                      for t in range(K_ROWS)]
            q_rows.append(jnp.concatenate(blocks, axis=-1))
        kinds.append(jnp.concatenate(q_rows, axis=-2))
    return jnp.stack(kinds, axis=1)


def _nbr_attn_kernel(q_ref, k_ref, v_ref, ga_ref, ck_ref, cv_ref, bias_ref, o_ref,
                     kc_ref, vaug_ref, p_ref, *, rows):
    n_groups = rows // Q_ROWS
    nq = Q_ROWS * GRID_W
    nk = K_ROWS * GRID_W
    seq = rows * GRID_W
    past = ck_ref.shape[0]

    kc_ref[...] = ck_ref[...].astype(BF16)
    vaug_ref[0:seq, 0:HEAD_DIM] = v_ref[...]
    vaug_ref[seq:seq + past, 0:HEAD_DIM] = cv_ref[...].astype(BF16)
    vaug_ref[:, HEAD_DIM:] = jnp.ones((seq + past, HEAD_DIM), BF16)

    def slab_start(g):
        base = jnp.clip(Q_ROWS * g - WIN_R // 2, 0, rows - K_ROWS)
        return pl.multiple_of(base * GRID_W, GRID_W)

    def scores(g, slot):
        kind = jnp.where(g == 0, 0, jnp.where(g == n_groups - 1, 2, 1))
        q = q_ref[pl.ds(pl.multiple_of(g * nq, nq), nq), :]
        s_loc = _dot_nt(q, k_ref[pl.ds(slab_start(g), nk), :]) + bias_ref[kind]
        s_ctx = _dot_nt(q, kc_ref[...])
        m = jnp.maximum(jnp.max(s_loc, axis=-1, keepdims=True),
                        jnp.max(s_ctx, axis=-1, keepdims=True))
        p_ref[slot, :, 0:nk] = jnp.exp2(s_loc - m).astype(BF16)
        p_ref[slot, :, nk:] = jnp.exp2(s_ctx - m).astype(BF16)

    def values(g, slot):
        q0 = pl.multiple_of(g * nq, nq)
        o = (jnp.dot(p_ref[slot, :, 0:nk], vaug_ref[pl.ds(slab_start(g), nk), :],
                     preferred_element_type=F32)
             + jnp.dot(p_ref[slot, :, nk:], vaug_ref[seq:seq + past, :],
                       preferred_element_type=F32))
        gate = ga_ref[pl.ds(q0, nq), :].astype(F32)
        o_ref[pl.ds(q0, nq), :] = (o[:, :HEAD_DIM] / o[:, HEAD_DIM:] * gate).astype(BF16)

    scores(0, 0)

    def step(i, carry):
        g = 2 * i + 1
        scores(g, 1)
        values(g - 1, 0)
        scores(g + 1, 0)
        values(g, 1)
        return carry

    lax.fori_loop(0, (n_groups - 2) // 2, step, 0)
    scores(n_groups - 1, 1)
    values(n_groups - 2, 0)
    values(n_groups - 1, 1)


def _nbr_attention(z, cache_k, cache_v, bias, layer, seq):
    t = z.shape[0]
    rows = seq // GRID_W
    assert (rows // Q_ROWS) % 2 == 0
    past = cache_k.shape[2]

    def col(cb):
        return pl.BlockSpec((seq, HEAD_DIM), lambda h, b: (b, cb * N_HEADS_A + h))

    depth = cache_k.shape[1]
    cache_k = cache_k.reshape(t // seq, depth, past, D_ATTN)
    cache_v = cache_v.reshape(t // seq, depth, past, D_ATTN)
    cache_spec = pl.BlockSpec((None, None, past, HEAD_DIM), lambda h, b: (b, layer, 0, h))
    return pl.pallas_call(
        functools.partial(_nbr_attn_kernel, rows=rows),
        grid=(N_HEADS_A, t // seq),
        in_specs=[col(COL_Q), col(COL_K), col(COL_V), col(COL_GA), cache_spec, cache_spec,
                  pl.BlockSpec((None, 3, Q_ROWS * GRID_W, K_ROWS * GRID_W),
                               lambda h, b: (h, 0, 0, 0))],
        out_specs=pl.BlockSpec((seq, HEAD_DIM), lambda h, b: (b, h)),
        out_shape=jax.ShapeDtypeStruct((t, D_ATTN), BF16),
        scratch_shapes=[pltpu.VMEM((past, HEAD_DIM), BF16),
                        pltpu.VMEM((seq + past, 2 * HEAD_DIM), BF16),
                        pltpu.VMEM((2, Q_ROWS * GRID_W, K_ROWS * GRID_W + past), BF16)],
        compiler_params=_params("arbitrary", "arbitrary"),
        name="nbr_attention",
    )(z, z, z, z, cache_k, cache_v, bias)


CONV_HALO = 16
CONV_CHUNK = 64
TAIL_BLOCK = 128


def _tail_kernel(x_ref, a_ref, u_ref, up_ref, un_ref, s_ref, sp_ref, sn_ref, gc_ref, gate_ref,
                 dww_ref, dwb_ref, lng_ref, lnb_ref, wp_ref, bp_ref, wo_ref, fg_ref, o_ref,
                 ext_ref, conv_ref, cat_ref, *, tm, tiles_per_seq, final):
    i = pl.program_id(0)
    first = (i % tiles_per_seq) == 0
    last = (i % tiles_per_seq) == tiles_per_seq - 1
    halo_tiles = CONV_HALO // SUBLANES
    n_tiles = tm // SUBLANES

    def glu(u, s):
        return u[...].astype(F32) * s[...].astype(F32)

    def tiles(a):
        return a.reshape(a.shape[0] // SUBLANES, SUBLANES, D_CONV)

    ext_ref[0:halo_tiles] = tiles(jnp.where(first, 0.0, glu(up_ref, sp_ref)))
    ext_ref[halo_tiles:halo_tiles + n_tiles] = tiles(glu(u_ref, s_ref))
    ext_ref[halo_tiles + n_tiles:] = tiles(jnp.where(last, 0.0, glu(un_ref, sn_ref)))
    cat_ref[:, :D_ATTN] = a_ref[...]

    chunk_tiles = CONV_CHUNK // SUBLANES
    block_tiles = TAIL_BLOCK // SUBLANES
    win_tiles = chunk_tiles + 2 * halo_tiles
    off = CONV_HALO - CONV_K // 2

    def conv_chunk(c0):
        for cb in range(D_CONV // LANES):
            lanes = slice(cb * LANES, (cb + 1) * LANES)
            win = ext_ref[c0:c0 + win_tiles, :, lanes].reshape(win_tiles * SUBLANES, LANES)
            acc = jnp.zeros((CONV_CHUNK, LANES), F32)
            for s in range(SUBLANES):
                taps = [k for k in range(CONV_K) if (k + off) % SUBLANES == s]
                if not taps:
                    continue
                sh = win if s == 0 else pltpu.roll(win, win_tiles * SUBLANES - s, axis=0)
                for k in taps:
                    a0 = k + off - s
                    acc = acc + sh[a0:a0 + CONV_CHUNK] * dww_ref[k:k + 1, lanes]
            conv_ref[c0:c0 + chunk_tiles, :, lanes] = acc.reshape(chunk_tiles, SUBLANES, LANES)

    for rb in range(tm // TAIL_BLOCK):
        t0 = rb * block_tiles
        rows = slice(rb * TAIL_BLOCK, (rb + 1) * TAIL_BLOCK)
        for ch in range(TAIL_BLOCK // CONV_CHUNK):
            conv_chunk(t0 + ch * chunk_tiles)
        a = conv_ref[t0:t0 + block_tiles].reshape(TAIL_BLOCK, D_CONV) + dwb_ref[...]
        mu = jnp.mean(a, axis=-1, keepdims=True)
        d = a - mu
        var = jnp.mean(d * d, axis=-1, keepdims=True)
        y = d * lax.rsqrt(var + EPS) * lng_ref[...] + lnb_ref[...]
        y = y * jax.nn.sigmoid(y)
        br = jnp.dot(y.astype(BF16), wp_ref[...], preferred_element_type=F32) + bp_ref[...]
        cat_ref[rows, D_ATTN:] = (br * gc_ref[rows, :].astype(F32)).astype(BF16)
        out = jnp.dot(cat_ref[rows, :], wo_ref[...], preferred_element_type=F32)
        x = x_ref[rows, :] + gate_ref[...] * out
        if final:
            x = x * lax.rsqrt(jnp.mean(x * x, axis=-1, keepdims=True) + EPS) * fg_ref[...]
        o_ref[rows, :] = x


def _tail(x, attn, z, mod, dw_w, dw_b, ln_g, ln_b, w_pw2, b_pw2, w_out, final_g, *,
          seq, tm, rows_per_mod, mod_row0, final):
    t = x.shape[0]
    tiles_per_seq = seq // tm
    tiles_per_mod = rows_per_mod // tm
    hb = tm // CONV_HALO
    n_hb = t // CONV_HALO

    def main(cb):
        return pl.BlockSpec((tm, D_CONV), lambda i: (i, cb))

    def before(cb):
        return pl.BlockSpec((CONV_HALO, D_CONV), lambda i: (jnp.maximum(i * hb - 1, 0), cb))

    def after(cb):
        return pl.BlockSpec((CONV_HALO, D_CONV),
                            lambda i: (jnp.minimum((i + 1) * hb, n_hb - 1), cb))

    def full(shape):
        return pl.BlockSpec(shape, lambda i: (0,) * len(shape))

    ext_tiles = (tm + 2 * CONV_HALO) // SUBLANES
    return pl.pallas_call(
        functools.partial(_tail_kernel, tm=tm, tiles_per_seq=tiles_per_seq, final=final),
        grid=(t // tm,),
        in_specs=[pl.BlockSpec((tm, D_MODEL), lambda i: (i, 0)),
                  pl.BlockSpec((tm, D_ATTN), lambda i: (i, 0)),
                  main(COL_U), before(COL_U), after(COL_U),
                  main(COL_UG), before(COL_UG), after(COL_UG), main(COL_GC),
                  pl.BlockSpec((None, 1, D_MODEL),
                               lambda i: ((mod_row0 + i // tiles_per_mod) * 3 + 2, 0, 0)),
                  full((CONV_K, D_CONV)), full((1, D_CONV)), full((1, D_CONV)), full((1, D_CONV)),
                  full((D_CONV, D_CONV)), full((1, D_CONV)),
                  full((D_MODEL, D_MODEL)), full((1, D_MODEL))],
        out_specs=pl.BlockSpec((tm, D_MODEL), lambda i: (i, 0)),
        out_shape=jax.ShapeDtypeStruct((t, D_MODEL), F32),
        scratch_shapes=[pltpu.VMEM((ext_tiles, SUBLANES, D_CONV), F32),
                        pltpu.VMEM((tm // SUBLANES, SUBLANES, D_CONV), F32),
                        pltpu.VMEM((tm, D_MODEL), BF16)],
        compiler_params=_params("arbitrary"),
        name="tail_final" if final else "tail",
    )(x, attn, z, z, z, z, z, z, z, mod, dw_w, dw_b.reshape(1, D_CONV), ln_g.reshape(1, D_CONV),
      ln_b.reshape(1, D_CONV), w_pw2, b_pw2.reshape(1, D_CONV), w_out,
      final_g.reshape(1, D_MODEL))


def kernel(x_prompt, x_sample, cache_k, cache_v, c, c_ctx, norm_g, w_ada, b_ada, w_in, b_in, rpb,
           dw_w, dw_b, cln_g, cln_b, w_pw2, b_pw2, w_out, final_norm_g):
    batch, seq, _ = x_prompt.shape
    dec_batch, dec_seq, _ = x_sample.shape
    assert dec_batch + 1 <= MOD_ROWS

    cvec = jnp.zeros((MOD_ROWS, D_MODEL), F32).at[:dec_batch].set(c).at[dec_batch].set(c_ctx)
    mod = _modulation(cvec, w_ada, b_ada).reshape(DEPTH, MOD_ROWS * 3, 1, D_MODEL)

    w_in_b = w_in.astype(BF16)
    w_pw2_b = w_pw2.astype(BF16)
    w_out_b = w_out.astype(BF16)

    xp = x_prompt.reshape(batch * seq, D_MODEL)
    xs = x_sample.reshape(dec_batch * dec_seq, D_MODEL)
    caches = None
    for l in range(DEPTH):
        final = l == DEPTH - 1
        tail_w = (mod[l], dw_w[l], dw_b[l], cln_g[l], cln_b[l], w_pw2_b[l], b_pw2[l], w_out_b[l],
                  final_norm_g)
        prompt_mod = dict(rows_per_mod=batch * seq, mod_row0=dec_batch)
        sample_mod = dict(rows_per_mod=dec_seq, mod_row0=0)

        zp, *caches = _inproj(xp, norm_g[l], mod[l], w_in_b[l], b_in[l], kv=(l, seq, caches),
                              **prompt_mod)
        zs, = _inproj(xs, norm_g[l], mod[l], w_in_b[l], b_in[l], **sample_mod)

        ap = _ctx_attention(zp, seq)
        bias = _nbr_bias_table(rpb[l], dec_seq // GRID_W)
        a_s = _nbr_attention(zs, cache_k, cache_v, bias, l, dec_seq)

        xp = _tail(xp, ap, zp, *tail_w, seq=seq, tm=seq, final=final, **prompt_mod)
        xs = _tail(xs, a_s, zs, *tail_w, seq=dec_seq, tm=512, final=final, **sample_mod)

    y_prompt = xp.reshape(batch, seq, D_MODEL)
    y_sample = xs.reshape(dec_batch, dec_seq, D_MODEL)
    new_k, new_v = (a.reshape(batch, DEPTH, seq, N_HEADS_A, HEAD_DIM) for a in caches)
    return (y_prompt, y_sample, new_k, new_v)
```

```python
import functools

import numpy as np
import jax
import jax.numpy as jnp
from jax import lax
from jax.experimental import pallas as pl
from jax.experimental.pallas import tpu as pltpu

D_MODEL = 2048
DEPTH = 2
GRID_W = 64
N_HEADS_A = 8
HEAD_DIM = 128
D_ATTN = N_HEADS_A * HEAD_DIM
D_CONV = D_MODEL - D_ATTN
D_IN = 4 * D_ATTN + 3 * D_CONV
WIN_R = 8
WIN_C = 16
CONV_K = 31
EPS = 1e-6

F32 = jnp.float32
BF16 = jnp.bfloat16

VMEM_LIMIT_BYTES = 56 * 1024 * 1024
LANES = 128
SUBLANES = 8

MOD_ROWS = 16
MASKED = -1e30
LOG2E = float(np.log2(np.e))
Q_SCALE = HEAD_DIM ** -0.5 * LOG2E

COL_Q, COL_K, COL_V, COL_GA, COL_U, COL_UG, COL_GC = range(7)

Q_ROWS = 4
K_ROWS = 12


def _params(*sem):
    return pltpu.CompilerParams(dimension_semantics=sem, vmem_limit_bytes=VMEM_LIMIT_BYTES)


def _mod_kernel(c_ref, w_ref, b_ref, o_ref):
    c = c_ref[...]
    s = c * jax.nn.sigmoid(c)
    o_ref[...] = jnp.dot(s.astype(BF16), w_ref[...].astype(BF16),
                         preferred_element_type=F32) + b_ref[...]


def _modulation(cvec, w_ada, b_ada):
    tn = 1536
    return pl.pallas_call(
        _mod_kernel,
        grid=(DEPTH, 3 * D_MODEL // tn),
        in_specs=[
            pl.BlockSpec((MOD_ROWS, D_MODEL), lambda l, j: (0, 0)),
            pl.BlockSpec((None, D_MODEL, tn), lambda l, j: (l, 0, j)),
            pl.BlockSpec((None, 1, tn), lambda l, j: (l, 0, j)),
        ],
        out_specs=pl.BlockSpec((None, MOD_ROWS, tn), lambda l, j: (l, 0, j)),
        out_shape=jax.ShapeDtypeStruct((DEPTH, MOD_ROWS, 3 * D_MODEL), F32),
        compiler_params=_params("arbitrary", "arbitrary"),
        name="modulation",
    )(cvec, w_ada, b_ada.reshape(DEPTH, 1, 3 * D_MODEL))


def _inproj_kernel(x_ref, g_ref, shift_ref, scl_ref, w_ref, b_ref, *rest, emit_kv, aliased):
    rest = rest[2:] if aliased else rest
    if emit_kv:
        z_ref, kc_ref, vc_ref, h_ref = rest
    else:
        z_ref, h_ref = rest
    j = pl.program_id(1)

    @pl.when(j == 0)
    def _():
        x = x_ref[...]
        y = x * lax.rsqrt(jnp.mean(x * x, axis=-1, keepdims=True) + EPS) * g_ref[...]
        h_ref[...] = (y * (1.0 + scl_ref[...]) + shift_ref[...]).astype(BF16)

    acc = jnp.dot(h_ref[...], w_ref[...], preferred_element_type=F32) + b_ref[...]
    if emit_kv:
        @pl.when(j == COL_K)
        def _():
            kc_ref[...] = acc.reshape(kc_ref.shape)

        @pl.when(j == COL_V)
        def _():
            vc_ref[...] = acc.reshape(vc_ref.shape)

    gate_col = (j == COL_GA) | (j == COL_UG) | (j == COL_GC)

    @pl.when(gate_col)
    def _():
        sig = jax.nn.sigmoid(acc)
        z_ref[...] = jnp.where(j == COL_UG, sig, acc * sig).astype(BF16)

    @pl.when(jnp.logical_not(gate_col))
    def _():
        z_ref[...] = (acc * jnp.where(j == COL_Q, Q_SCALE, 1.0)).astype(BF16)


def _inproj(x, norm_g, mod, w_in, b_in, *, rows_per_mod, mod_row0, kv=None):
    t = x.shape[0]
    tm, tn = 1024, 1024
    tiles_per_mod = rows_per_mod // tm

    def mod_idx(part):
        return lambda i, j: ((mod_row0 + i // tiles_per_mod) * 3 + part, 0, 0)

    args = [x, norm_g.reshape(1, D_MODEL), mod, mod, w_in, b_in.reshape(1, D_IN)]
    x_mode = dict(pipeline_mode=pl.Buffered(1)) if kv is not None else {}
    in_specs = [
        pl.BlockSpec((tm, D_MODEL), lambda i, j: (i, 0), **x_mode),
        pl.BlockSpec((1, D_MODEL), lambda i, j: (0, 0)),
        pl.BlockSpec((None, 1, D_MODEL), mod_idx(0)),
        pl.BlockSpec((None, 1, D_MODEL), mod_idx(1)),
        pl.BlockSpec((D_MODEL, tn), lambda i, j: (0, j)),
        pl.BlockSpec((1, tn), lambda i, j: (0, j)),
    ]
    out_shape = [jax.ShapeDtypeStruct((t, D_IN), BF16)]
    out_specs = [pl.BlockSpec((tm, tn), lambda i, j: (i, j))]
    aliases = {}
    if kv is not None:
        layer, seq, caches = kv
        assert tm % seq == 0
        cache_shape = jax.ShapeDtypeStruct((t // seq, DEPTH, seq, D_ATTN), F32)
        cache_spec = pl.BlockSpec((tm // seq, None, seq, D_ATTN), lambda i, j: (i, layer, 0, 0))
        out_shape += [cache_shape, cache_shape]
        out_specs += [cache_spec, cache_spec]
        if caches is not None:
            aliases = {len(args): 1, len(args) + 1: 2}
            args += list(caches)
            in_specs += [pl.BlockSpec(memory_space=pl.ANY)] * 2
    return pl.pallas_call(
        functools.partial(_inproj_kernel, emit_kv=kv is not None, aliased=bool(aliases)),
        grid=(t // tm, D_IN // tn),
        in_specs=in_specs,
        out_specs=out_specs,
        out_shape=out_shape,
        input_output_aliases=aliases,
        scratch_shapes=[pltpu.VMEM((tm, D_MODEL), BF16)],
        compiler_params=_params("arbitrary", "arbitrary"),
        name="inproj_kv" if kv is not None else "inproj",
    )(*args)


def _dot_nt(a, b):
    return lax.dot_general(a, b, (((1,), (1,)), ((), ())), preferred_element_type=F32)


def _ctx_attn_kernel(q_ref, k_ref, v_ref, ga_ref, o_ref):
    for h in range(N_HEADS_A):
        cols = slice(h * HEAD_DIM, (h + 1) * HEAD_DIM)
        s = _dot_nt(q_ref[:, cols], k_ref[:, cols])
        p = jnp.exp2(s - jnp.max(s, axis=-1, keepdims=True))
        denom = jnp.sum(p, axis=-1, keepdims=True)
        o = jnp.dot(p.astype(BF16), v_ref[:, cols], preferred_element_type=F32)
        o_ref[:, cols] = (o / denom * ga_ref[:, cols].astype(F32)).astype(BF16)


def _ctx_attention(z, seq):
    t = z.shape[0]

    def col(cb):
        return pl.BlockSpec((seq, D_ATTN), lambda b: (b, cb))

    return pl.pallas_call(
        _ctx_attn_kernel,
        grid=(t // seq,),
        in_specs=[col(COL_Q), col(COL_K), col(COL_V), col(COL_GA)],
        out_specs=pl.BlockSpec((seq, D_ATTN), lambda b: (b, 0)),
        out_shape=jax.ShapeDtypeStruct((t, D_ATTN), BF16),
        compiler_params=_params("arbitrary"),
        name="ctx_attention",
    )(z, z, z, z)


def _nbr_bias_table(rpb_l, rows):
    n_groups = rows // Q_ROWS
    j = np.arange(GRID_W)[:, None]
    cc = np.arange(GRID_W)[None, :]
    cs = np.clip(j - WIN_C // 2, 0, GRID_W - WIN_C)
    valid_c = (cc >= cs) & (cc < cs + WIN_C)
    dc = cc - j + WIN_C - 1
    onehot = ((dc[None] == np.arange(2 * WIN_C - 1)[:, None, None]) & valid_c[None])
    toe = jnp.einsum('hrd,djc->hrjc', rpb_l, onehot.astype(np.float32),
                     precision=lax.Precision.HIGHEST)
    toe = jnp.where(valid_c, toe * LOG2E, MASKED).astype(F32)
    masked = jnp.full((N_HEADS_A, GRID_W, GRID_W), MASKED, F32)
    kinds = []
    for g in (0, 1, n_groups - 1):
        base = min(max(Q_ROWS * g - WIN_R // 2, 0), rows - K_ROWS)
        q_rows = []
        for a in range(Q_ROWS):
            i = Q_ROWS * g + a
            rs = min(max(i - WIN_R // 2, 0), rows - WIN_R)
            blocks = [toe[:, base + t - i + WIN_R - 1] if rs <= base + t < rs + WIN_R else masked
                      for t in range(K_ROWS)]
            q_rows.append(jnp.concatenate(blocks, axis=-1))
        kinds.append(jnp.concatenate(q_rows, axis=-2))
    return jnp.stack(kinds, axis=1)


def _nbr_attn_kernel(q_ref, k_ref, v_ref, ga_ref, ck_ref, cv_ref, bias_ref, o_ref,
                     kc_ref, vaug_ref, p_ref, *, rows):
    n_groups = rows // Q_ROWS
    nq = Q_ROWS * GRID_W
    nk = K_ROWS * GRID_W
    seq = rows * GRID_W
    past = ck_ref.shape[0]

    kc_ref[...] = ck_ref[...].astype(BF16)
    vaug_ref[0:seq, 0:HEAD_DIM] = v_ref[...]
    vaug_ref[seq:seq + past, 0:HEAD_DIM] = cv_ref[...].astype(BF16)
    vaug_ref[:, HEAD_DIM:] = jnp.ones((seq + past, HEAD_DIM), BF16)

    def slab_start(g):
        base = jnp.clip(Q_ROWS * g - WIN_R // 2, 0, rows - K_ROWS)
        return pl.multiple_of(base * GRID_W, GRID_W)

    def scores(g, slot):
        kind = jnp.where(g == 0, 0, jnp.where(g == n_groups - 1, 2, 1))
        q = q_ref[pl.ds(pl.multiple_of(g * nq, nq), nq), :]
        s_loc = _dot_nt(q, k_ref[pl.ds(slab_start(g), nk), :]) + bias_ref[kind]
        s_ctx = _dot_nt(q, kc_ref[...])
        m = jnp.maximum(jnp.max(s_loc, axis=-1, keepdims=True),
                        jnp.max(s_ctx, axis=-1, keepdims=True))
        p_ref[slot, :, 0:nk] = jnp.exp2(s_loc - m).astype(BF16)
        p_ref[slot, :, nk:] = jnp.exp2(s_ctx - m).astype(BF16)

    def values(g, slot):
        q0 = pl.multiple_of(g * nq, nq)
        o = (jnp.dot(p_ref[slot, :, 0:nk], vaug_ref[pl.ds(slab_start(g), nk), :],
                     preferred_element_type=F32)
             + jnp.dot(p_ref[slot, :, nk:], vaug_ref[seq:seq + past, :],
                       preferred_element_type=F32))
        gate = ga_ref[pl.ds(q0, nq), :].astype(F32)
        o_ref[pl.ds(q0, nq), :] = (o[:, :HEAD_DIM] / o[:, HEAD_DIM:] * gate).astype(BF16)

    def pair(k, score_slots, value_slots):
        scores(2 * k, score_slots[0])
        values(2 * k - 2, value_slots[0])
        scores(2 * k + 1, score_slots[1])
        values(2 * k - 1, value_slots[1])

    n_pairs = n_groups // 2
    scores(0, 0)
    scores(1, 1)

    def step(i, carry):
        pair(2 * i + 1, (2, 3), (0, 1))
        pair(2 * i + 2, (0, 1), (2, 3))
        return carry

    lax.fori_loop(0, (n_pairs - 2) // 2, step, 0)
    pair(n_pairs - 1, (2, 3), (0, 1))
    values(n_groups - 2, 2)
    values(n_groups - 1, 3)


def _nbr_attention(z, cache_k, cache_v, bias, layer, seq):
    t = z.shape[0]
    rows = seq // GRID_W
    assert (rows // Q_ROWS) % 4 == 0
    past = cache_k.shape[2]

    def col(cb):
        return pl.BlockSpec((seq, HEAD_DIM), lambda h, b: (b, cb * N_HEADS_A + h))

    depth = cache_k.shape[1]
    cache_k = cache_k.reshape(t // seq, depth, past, D_ATTN)
    cache_v = cache_v.reshape(t // seq, depth, past, D_ATTN)
    cache_spec = pl.BlockSpec((None, None, past, HEAD_DIM), lambda h, b: (b, layer, 0, h))
    return pl.pallas_call(
        functools.partial(_nbr_attn_kernel, rows=rows),
        grid=(N_HEADS_A, t // seq),
        in_specs=[col(COL_Q), col(COL_K), col(COL_V), col(COL_GA), cache_spec, cache_spec,
                  pl.BlockSpec((None, 3, Q_ROWS * GRID_W, K_ROWS * GRID_W),
                               lambda h, b: (h, 0, 0, 0))],
        out_specs=pl.BlockSpec((seq, HEAD_DIM), lambda h, b: (b, h)),
        out_shape=jax.ShapeDtypeStruct((t, D_ATTN), BF16),
        scratch_shapes=[pltpu.VMEM((past, HEAD_DIM), BF16),
                        pltpu.VMEM((seq + past, 2 * HEAD_DIM), BF16),
                        pltpu.VMEM((4, Q_ROWS * GRID_W, K_ROWS * GRID_W + past), BF16)],
        compiler_params=_params("arbitrary", "arbitrary"),
        name="nbr_attention",
    )(z, z, z, z, cache_k, cache_v, bias)


CONV_HALO = 16
CONV_CHUNK = 64


def _conv_kernel(u_ref, up_ref, un_ref, s_ref, sp_ref, sn_ref, gc_ref, dww_ref, dwb_ref,
                 lng_ref, lnb_ref, w_ref, b_ref, o_ref, ext_ref, conv_ref, *, tm, tiles_per_seq):
    i = pl.program_id(0)
    first = (i % tiles_per_seq) == 0
    last = (i % tiles_per_seq) == tiles_per_seq - 1
    halo_tiles = CONV_HALO // SUBLANES
    n_tiles = tm // SUBLANES

    def glu(u, s):
        return u[...].astype(F32) * s[...].astype(F32)

    def tiles(a):
        return a.reshape(a.shape[0] // SUBLANES, SUBLANES, D_CONV)

    ext_ref[0:halo_tiles] = tiles(jnp.where(first, 0.0, glu(up_ref, sp_ref)))
    ext_ref[halo_tiles:halo_tiles + n_tiles] = tiles(glu(u_ref, s_ref))
    ext_ref[halo_tiles + n_tiles:] = tiles(jnp.where(last, 0.0, glu(un_ref, sn_ref)))

    chunk_tiles = CONV_CHUNK // SUBLANES
    win_tiles = chunk_tiles + 2 * halo_tiles
    off = CONV_HALO - CONV_K // 2

    def chunk(cidx, carry):
        c0 = cidx * chunk_tiles
        for cb in range(D_CONV // LANES):
            lanes = slice(cb * LANES, (cb + 1) * LANES)
            win = ext_ref[pl.ds(c0, win_tiles), :, lanes].reshape(win_tiles * SUBLANES, LANES)
            acc = jnp.zeros((CONV_CHUNK, LANES), F32)
            for s in range(SUBLANES):
                taps = [k for k in range(CONV_K) if (k + off) % SUBLANES == s]
                if not taps:
                    continue
                sh = win if s == 0 else pltpu.roll(win, win_tiles * SUBLANES - s, axis=0)
                for k in taps:
                    a0 = k + off - s
                    acc = acc + sh[a0:a0 + CONV_CHUNK] * dww_ref[k:k + 1, lanes]
            conv_ref[pl.ds(c0, chunk_tiles), :, lanes] = acc.reshape(chunk_tiles, SUBLANES, LANES)
        return carry

    lax.fori_loop(0, tm // CONV_CHUNK, chunk, 0)

    a = conv_ref[...].reshape(tm, D_CONV) + dwb_ref[...]
    mu = jnp.mean(a, axis=-1, keepdims=True)
    d = a - mu
    var = jnp.mean(d * d, axis=-1, keepdims=True)
    y = d * lax.rsqrt(var + EPS) * lng_ref[...] + lnb_ref[...]
    y = y * jax.nn.sigmoid(y)
    out = jnp.dot(y.astype(BF16), w_ref[...], preferred_element_type=F32) + b_ref[...]
    o_ref[...] = (out * gc_ref[...].astype(F32)).astype(BF16)


def _conformer_conv(z, dw_w, dw_b, ln_g, ln_b, w_pw2, b_pw2, *, seq, tm):
    t = z.shape[0]
    tiles_per_seq = seq // tm
    hb = tm // CONV_HALO
    n_hb = t // CONV_HALO

    def main(cb):
        return pl.BlockSpec((tm, D_CONV), lambda i: (i, cb))

    def before(cb):
        return pl.BlockSpec((CONV_HALO, D_CONV), lambda i: (jnp.maximum(i * hb - 1, 0), cb))

    def after(cb):
        return pl.BlockSpec((CONV_HALO, D_CONV),
                            lambda i: (jnp.minimum((i + 1) * hb, n_hb - 1), cb))

    def full(shape):
        return pl.BlockSpec(shape, lambda i: (0,) * len(shape))

    ext_tiles = (tm + 2 * CONV_HALO) // SUBLANES
    return pl.pallas_call(
        functools.partial(_conv_kernel, tm=tm, tiles_per_seq=tiles_per_seq),
        grid=(t // tm,),
        in_specs=[main(COL_U), before(COL_U), after(COL_U),
                  main(COL_UG), before(COL_UG), after(COL_UG), main(COL_GC),
                  full((CONV_K, D_CONV)), full((1, D_CONV)), full((1, D_CONV)), full((1, D_CONV)),
                  full((D_CONV, D_CONV)), full((1, D_CONV))],
        out_specs=pl.BlockSpec((tm, D_CONV), lambda i: (i, 0)),
        out_shape=jax.ShapeDtypeStruct((t, D_CONV), BF16),
        scratch_shapes=[pltpu.VMEM((ext_tiles, SUBLANES, D_CONV), F32),
                        pltpu.VMEM((tm // SUBLANES, SUBLANES, D_CONV), F32)],
        compiler_params=_params("arbitrary"),
        name="conformer_conv",
    )(z, z, z, z, z, z, z, dw_w, dw_b.reshape(1, D_CONV), ln_g.reshape(1, D_CONV),
      ln_b.reshape(1, D_CONV), w_pw2, b_pw2.reshape(1, D_CONV))


def _outproj_kernel(x_ref, a_ref, c_ref, gate_ref, w_ref, fg_ref, o_ref, cat_ref, *, final):
    cat_ref[:, :D_ATTN] = a_ref[...]
    cat_ref[:, D_ATTN:] = c_ref[...]
    out = jnp.dot(cat_ref[...], w_ref[...], preferred_element_type=F32)
    x = x_ref[...] + gate_ref[...] * out
    if final:
        x = x * lax.rsqrt(jnp.mean(x * x, axis=-1, keepdims=True) + EPS) * fg_ref[...]
    o_ref[...] = x


def _outproj(x, attn, conv, mod, w_out, final_g, *, rows_per_mod, mod_row0, final):
    t = x.shape[0]
    tm = 512
    tiles_per_mod = rows_per_mod // tm
    return pl.pallas_call(
        functools.partial(_outproj_kernel, final=final),
        grid=(t // tm,),
        in_specs=[
            pl.BlockSpec((tm, D_MODEL), lambda i: (i, 0)),
            pl.BlockSpec((tm, D_ATTN), lambda i: (i, 0)),
            pl.BlockSpec((tm, D_CONV), lambda i: (i, 0)),
            pl.BlockSpec((None, 1, D_MODEL),
                         lambda i: ((mod_row0 + i // tiles_per_mod) * 3 + 2, 0, 0)),
            pl.BlockSpec((D_MODEL, D_MODEL), lambda i: (0, 0)),
            pl.BlockSpec((1, D_MODEL), lambda i: (0, 0)),
        ],
        out_specs=pl.BlockSpec((tm, D_MODEL), lambda i: (i, 0)),
        out_shape=jax.ShapeDtypeStruct((t, D_MODEL), F32),
        scratch_shapes=[pltpu.VMEM((tm, D_MODEL), BF16)],
        compiler_params=_params("arbitrary"),
        name="outproj_final" if final else "outproj",
    )(x, attn, conv, mod, w_out, final_g.reshape(1, D_MODEL))


def kernel(x_prompt, x_sample, cache_k, cache_v, c, c_ctx, norm_g, w_ada, b_ada, w_in, b_in, rpb,
           dw_w, dw_b, cln_g, cln_b, w_pw2, b_pw2, w_out, final_norm_g):
    batch, seq, _ = x_prompt.shape
    dec_batch, dec_seq, _ = x_sample.shape
    assert dec_batch + 1 <= MOD_ROWS

    cvec = jnp.zeros((MOD_ROWS, D_MODEL), F32).at[:dec_batch].set(c).at[dec_batch].set(c_ctx)
    mod = _modulation(cvec, w_ada, b_ada).reshape(DEPTH, MOD_ROWS * 3, 1, D_MODEL)

    xp = x_prompt.reshape(batch * seq, D_MODEL)
    xs = x_sample.reshape(dec_batch * dec_seq, D_MODEL)
    caches = None
    for l in range(DEPTH):
        final = l == DEPTH - 1
        w_in_l, w_out_l = w_in[l].astype(BF16), w_out[l].astype(BF16)
        conv_w = (dw_w[l], dw_b[l], cln_g[l], cln_b[l], w_pw2[l].astype(BF16), b_pw2[l])
        prompt_mod = dict(rows_per_mod=batch * seq, mod_row0=dec_batch)
        sample_mod = dict(rows_per_mod=dec_seq, mod_row0=0)

        zp, *caches = _inproj(xp, norm_g[l], mod[l], w_in_l, b_in[l], kv=(l, seq, caches),
                              **prompt_mod)
        zs, = _inproj(xs, norm_g[l], mod[l], w_in_l, b_in[l], **sample_mod)

        ap = _ctx_attention(zp, seq)
        bias = _nbr_bias_table(rpb[l], dec_seq // GRID_W)
        a_s = _nbr_attention(zs, cache_k, cache_v, bias, l, dec_seq)

        cp = _conformer_conv(zp, *conv_w, seq=seq, tm=seq)
        cs = _conformer_conv(zs, *conv_w, seq=dec_seq, tm=512)

        xp = _outproj(xp, ap, cp, mod[l], w_out_l, final_norm_g, final=final, **prompt_mod)
        xs = _outproj(xs, a_s, cs, mod[l], w_out_l, final_norm_g, final=final, **sample_mod)

    y_prompt = xp.reshape(batch, seq, D_MODEL)
    y_sample = xs.reshape(dec_batch, dec_seq, D_MODEL)
    new_k, new_v = (a.reshape(batch, DEPTH, seq, N_HEADS_A, HEAD_DIM) for a in caches)
    return (y_prompt, y_sample, new_k, new_v)
```

```python
import functools

import numpy as np
import jax
import jax.numpy as jnp
from jax import lax
from jax.experimental import pallas as pl
from jax.experimental.pallas import tpu as pltpu

D_MODEL = 2048
DEPTH = 2
GRID_W = 64
N_HEADS_A = 8
HEAD_DIM = 128
D_ATTN = N_HEADS_A * HEAD_DIM
D_CONV = D_MODEL - D_ATTN
D_IN = 4 * D_ATTN + 3 * D_CONV
WIN_R = 8
WIN_C = 16
CONV_K = 31
EPS = 1e-6

F32 = jnp.float32
BF16 = jnp.bfloat16

VMEM_LIMIT_BYTES = 56 * 1024 * 1024
LANES = 128
SUBLANES = 8

MOD_ROWS = 16
MASKED = -1e30
LOG2E = float(np.log2(np.e))
Q_SCALE = HEAD_DIM ** -0.5 * LOG2E

COL_Q, COL_K, COL_V, COL_GA, COL_U, COL_UG, COL_GC = range(7)

Q_ROWS = 4
K_ROWS = 12


def _sigmoid(x):
    return 0.5 * jnp.tanh(0.5 * x) + 0.5


def _params(*sem):
    return pltpu.CompilerParams(dimension_semantics=sem, vmem_limit_bytes=VMEM_LIMIT_BYTES)


def _mod_kernel(c_ref, w_ref, b_ref, o_ref):
    c = c_ref[...]
    s = c * jax.nn.sigmoid(c)
    o_ref[...] = jnp.dot(s.astype(BF16), w_ref[...].astype(BF16),
                         preferred_element_type=F32) + b_ref[...]


def _modulation(cvec, w_ada, b_ada):
    tn = 1536
    return pl.pallas_call(
        _mod_kernel,
        grid=(DEPTH, 3 * D_MODEL // tn),
        in_specs=[
            pl.BlockSpec((MOD_ROWS, D_MODEL), lambda l, j: (0, 0)),
            pl.BlockSpec((None, D_MODEL, tn), lambda l, j: (l, 0, j)),
            pl.BlockSpec((None, 1, tn), lambda l, j: (l, 0, j)),
        ],
        out_specs=pl.BlockSpec((None, MOD_ROWS, tn), lambda l, j: (l, 0, j)),
        out_shape=jax.ShapeDtypeStruct((DEPTH, MOD_ROWS, 3 * D_MODEL), F32),
        compiler_params=_params("arbitrary", "arbitrary"),
        name="modulation",
    )(cvec, w_ada, b_ada.reshape(DEPTH, 1, 3 * D_MODEL))


NORM_ROWS = 128

def _inproj_kernel(x_ref, g_ref, shift_ref, scl_ref, w_ref, b_ref, *rest, emit_kv):
    if emit_kv:
        _, _, z_ref, kc_ref, vc_ref, h_ref = rest
    else:
        z_ref, h_ref = rest
    j = pl.program_id(1)

    @pl.when(j == 0)
    def _():
        gain = g_ref[...] * (1.0 + scl_ref[...])
        shift = shift_ref[...]

        def norm_rows(r, carry):
            rows = pl.ds(pl.multiple_of(r * NORM_ROWS, NORM_ROWS), NORM_ROWS)
            x = x_ref[rows, :]
            inv = lax.rsqrt(jnp.mean(x * x, axis=-1, keepdims=True) + EPS)
            h_ref[rows, :] = (x_ref[rows, :] * inv * gain + shift).astype(BF16)
            return carry

        lax.fori_loop(0, x_ref.shape[0] // NORM_ROWS, norm_rows, 0)

    acc = jnp.dot(h_ref[...], w_ref[...], preferred_element_type=F32) + b_ref[...]
    if emit_kv:
        @pl.when(j == COL_K)
        def _():
            kc_ref[...] = acc.reshape(kc_ref.shape)

        @pl.when(j == COL_V)
        def _():
            vc_ref[...] = acc.reshape(vc_ref.shape)

    gate_col = (j == COL_GA) | (j == COL_UG) | (j == COL_GC)

    @pl.when(gate_col)
    def _():
        sig = _sigmoid(acc)
        z_ref[...] = jnp.where(j == COL_UG, sig, acc * sig).astype(BF16)

    @pl.when(jnp.logical_not(gate_col))
    def _():
        z_ref[...] = (acc * jnp.where(j == COL_Q, Q_SCALE, 1.0)).astype(BF16)


def _inproj(x, norm_g, mod, w_in, b_in, *, rows_per_mod, mod_row0, kv=None):
    t = x.shape[0]
    tm, tn = 1024, 1024
    tiles_per_mod = rows_per_mod // tm

    def mod_idx(part):
        return lambda i, j: ((mod_row0 + i // tiles_per_mod) * 3 + part, 0, 0)

    args = [x, norm_g.reshape(1, D_MODEL), mod, mod, w_in, b_in.reshape(1, D_IN)]
    x_mode = dict(pipeline_mode=pl.Buffered(1)) if kv is not None else {}
    in_specs = [
        pl.BlockSpec((tm, D_MODEL), lambda i, j: (i, 0), **x_mode),
        pl.BlockSpec((1, D_MODEL), lambda i, j: (0, 0)),
        pl.BlockSpec((None, 1, D_MODEL), mod_idx(0)),
        pl.BlockSpec((None, 1, D_MODEL), mod_idx(1)),
        pl.BlockSpec((D_MODEL, tn), lambda i, j: (0, j)),
        pl.BlockSpec((1, tn), lambda i, j: (0, j)),
    ]
    out_shape = [jax.ShapeDtypeStruct((t, D_IN), BF16)]
    out_specs = [pl.BlockSpec((tm, tn), lambda i, j: (i, j))]
    aliases = {}
    if kv is not None:
        layer, seq, caches = kv
        assert tm % seq == 0
        cache_shape = jax.ShapeDtypeStruct((t // seq, DEPTH, seq, D_ATTN), F32)
        cache_spec = pl.BlockSpec((tm // seq, None, seq, D_ATTN), lambda i, j: (i, layer, 0, 0))
        out_shape += [cache_shape, cache_shape]
        out_specs += [cache_spec, cache_spec]
        aliases = {len(args): 1, len(args) + 1: 2}
        args += list(caches)
        in_specs += [pl.BlockSpec(memory_space=pl.ANY)] * 2
    return pl.pallas_call(
        functools.partial(_inproj_kernel, emit_kv=kv is not None),
        grid=(t // tm, D_IN // tn),
        in_specs=in_specs,
        out_specs=out_specs,
        out_shape=out_shape,
        input_output_aliases=aliases,
        scratch_shapes=[pltpu.VMEM((tm, D_MODEL), BF16)],
        compiler_params=_params("arbitrary", "arbitrary"),
        name="inproj_kv" if kv is not None else "inproj",
    )(*args)


def _dot_nt(a, b):
    return lax.dot_general(a, b, (((1,), (1,)), ((), ())), preferred_element_type=F32)


def _ctx_attn_kernel(q_ref, k_ref, v_ref, ga_ref, o_ref):
    for h in range(N_HEADS_A):
        cols = slice(h * HEAD_DIM, (h + 1) * HEAD_DIM)
        s = _dot_nt(q_ref[:, cols], k_ref[:, cols])
        p = jnp.exp2(s - jnp.max(s, axis=-1, keepdims=True))
        denom = jnp.sum(p, axis=-1, keepdims=True)
        o = jnp.dot(p.astype(BF16), v_ref[:, cols], preferred_element_type=F32)
        o_ref[:, cols] = (o / denom * ga_ref[:, cols].astype(F32)).astype(BF16)


def _ctx_attention(z, seq):
    t = z.shape[0]

    def col(cb):
        return pl.BlockSpec((seq, D_ATTN), lambda b: (b, cb))

    return pl.pallas_call(
        _ctx_attn_kernel,
        grid=(t // seq,),
        in_specs=[col(COL_Q), col(COL_K), col(COL_V), col(COL_GA)],
        out_specs=pl.BlockSpec((seq, D_ATTN), lambda b: (b, 0)),
        out_shape=jax.ShapeDtypeStruct((t, D_ATTN), BF16),
        compiler_params=_params("arbitrary"),
        name="ctx_attention",
    )(z, z, z, z)


def _nbr_bias_table(rpb_l, rows):
    n_groups = rows // Q_ROWS
    j = np.arange(GRID_W)[:, None]
    cc = np.arange(GRID_W)[None, :]
    cs = np.clip(j - WIN_C // 2, 0, GRID_W - WIN_C)
    valid_c = (cc >= cs) & (cc < cs + WIN_C)
    dc = cc - j + WIN_C - 1
    onehot = ((dc[None] == np.arange(2 * WIN_C - 1)[:, None, None]) & valid_c[None])
    toe = jnp.einsum('hrd,djc->hrjc', rpb_l, onehot.astype(np.float32),
                     precision=lax.Precision.HIGHEST)
    toe = jnp.where(valid_c, toe * LOG2E, MASKED).astype(F32)
    masked = jnp.full((N_HEADS_A, GRID_W, GRID_W), MASKED, F32)
    kinds = []
    for g in (0, 1, n_groups - 1):
        base = min(max(Q_ROWS * g - WIN_R // 2, 0), rows - K_ROWS)
        q_rows = []
        for a in range(Q_ROWS):
            i = Q_ROWS * g + a
            rs = min(max(i - WIN_R // 2, 0), rows - WIN_R)
            blocks = [toe[:, base + t - i + WIN_R - 1] if rs <= base + t < rs + WIN_R else masked
                      for t in range(K_ROWS)]
            q_rows.append(jnp.concatenate(blocks, axis=-1))
        kinds.append(jnp.concatenate(q_rows, axis=-2))
    return jnp.stack(kinds, axis=1)


def _nbr_attn_kernel(q_ref, k_ref, v_ref, ga_ref, ck_ref, cv_ref, bias_ref, o_ref,
                     kc_ref, vaug_ref, p_ref, *, rows):
    n_groups = rows // Q_ROWS
    nq = Q_ROWS * GRID_W
    nk = K_ROWS * GRID_W
    seq = rows * GRID_W
    past = ck_ref.shape[0]

    kc_ref[...] = ck_ref[...].astype(BF16)
    vaug_ref[0:seq, 0:HEAD_DIM] = v_ref[...]
    vaug_ref[seq:seq + past, 0:HEAD_DIM] = cv_ref[...].astype(BF16)
    vaug_ref[:, HEAD_DIM:] = jnp.ones((seq + past, HEAD_DIM), BF16)

    def slab_start(g):
        base = jnp.clip(Q_ROWS * g - WIN_R // 2, 0, rows - K_ROWS)
        return pl.multiple_of(base * GRID_W, GRID_W)

    def scores(g, slot):
        kind = jnp.where(g == 0, 0, jnp.where(g == n_groups - 1, 2, 1))
        q = q_ref[pl.ds(pl.multiple_of(g * nq, nq), nq), :]
        s_loc = _dot_nt(q, k_ref[pl.ds(slab_start(g), nk), :]) + bias_ref[kind]
        s_ctx = _dot_nt(q, kc_ref[...])
        m = jnp.maximum(jnp.max(s_loc, axis=-1, keepdims=True),
                        jnp.max(s_ctx, axis=-1, keepdims=True))
        p_ref[slot, :, 0:nk] = jnp.exp2(s_loc - m).astype(BF16)
        p_ref[slot, :, nk:] = jnp.exp2(s_ctx - m).astype(BF16)

    def values(g, slot):
        q0 = pl.multiple_of(g * nq, nq)
        o = (jnp.dot(p_ref[slot, :, 0:nk], vaug_ref[pl.ds(slab_start(g), nk), :],
                     preferred_element_type=F32)
             + jnp.dot(p_ref[slot, :, nk:], vaug_ref[seq:seq + past, :],
                       preferred_element_type=F32))
        gate = ga_ref[pl.ds(q0, nq), :].astype(F32)
        o_ref[pl.ds(q0, nq), :] = (o[:, :HEAD_DIM] / o[:, HEAD_DIM:] * gate).astype(BF16)

    def pair(k, score_slots, value_slots):
        scores(2 * k, score_slots[0])
        values(2 * k - 2, value_slots[0])
        scores(2 * k + 1, score_slots[1])
        values(2 * k - 1, value_slots[1])

    n_pairs = n_groups // 2
    scores(0, 0)
    scores(1, 1)

    def step(i, carry):
        pair(3 * i + 1, (2, 3), (0, 1))
        pair(3 * i + 2, (4, 5), (2, 3))
        pair(3 * i + 3, (0, 1), (4, 5))
        return carry

    lax.fori_loop(0, (n_pairs - 2) // 3, step, 0)
    pair(n_pairs - 1, (2, 3), (0, 1))
    values(n_groups - 2, 2)
    values(n_groups - 1, 3)


def _nbr_attention(z, cache_k, cache_v, bias, layer, seq):
    t = z.shape[0]
    rows = seq // GRID_W
    n_groups = rows // Q_ROWS
    assert n_groups % 2 == 0 and (n_groups // 2 - 2) % 3 == 0
    past = cache_k.shape[2]

    def col(cb):
        return pl.BlockSpec((seq, HEAD_DIM), lambda h, b: (b, cb * N_HEADS_A + h))

    depth = cache_k.shape[1]
    cache_k = cache_k.reshape(t // seq, depth, past, D_ATTN)
    cache_v = cache_v.reshape(t // seq, depth, past, D_ATTN)
    cache_spec = pl.BlockSpec((None, None, past, HEAD_DIM), lambda h, b: (b, layer, 0, h))
    return pl.pallas_call(
        functools.partial(_nbr_attn_kernel, rows=rows),
        grid=(N_HEADS_A, t // seq),
        in_specs=[col(COL_Q), col(COL_K), col(COL_V), col(COL_GA), cache_spec, cache_spec,
                  pl.BlockSpec((None, 3, Q_ROWS * GRID_W, K_ROWS * GRID_W),
                               lambda h, b: (h, 0, 0, 0))],
        out_specs=pl.BlockSpec((seq, HEAD_DIM), lambda h, b: (b, h)),
        out_shape=jax.ShapeDtypeStruct((t, D_ATTN), BF16),
        scratch_shapes=[pltpu.VMEM((past, HEAD_DIM), BF16),
                        pltpu.VMEM((seq + past, 2 * HEAD_DIM), BF16),
                        pltpu.VMEM((6, Q_ROWS * GRID_W, K_ROWS * GRID_W + past), BF16)],
        compiler_params=_params("arbitrary", "arbitrary"),
        name="nbr_attention",
    )(z, z, z, z, cache_k, cache_v, bias)


CONV_HALO = 16
CONV_CHUNK = 64


def _conv_kernel(u_ref, up_ref, un_ref, s_ref, sp_ref, sn_ref, gc_ref, dww_ref, dwb_ref,
                 lng_ref, lnb_ref, w_ref, b_ref, o_ref, ext_ref, conv_ref, *, tm, tiles_per_seq):
    i = pl.program_id(0)
    first = (i % tiles_per_seq) == 0
    last = (i % tiles_per_seq) == tiles_per_seq - 1
    halo_tiles = CONV_HALO // SUBLANES
    n_tiles = tm // SUBLANES

    def glu(u, s):
        return u[...].astype(F32) * s[...].astype(F32)

    def tiles(a):
        return a.reshape(a.shape[0] // SUBLANES, SUBLANES, D_CONV)

    ext_ref[0:halo_tiles] = tiles(jnp.where(first, 0.0, glu(up_ref, sp_ref)))
    ext_ref[halo_tiles:halo_tiles + n_tiles] = tiles(glu(u_ref, s_ref))
    ext_ref[halo_tiles + n_tiles:] = tiles(jnp.where(last, 0.0, glu(un_ref, sn_ref)))

    chunk_tiles = CONV_CHUNK // SUBLANES
    win_tiles = chunk_tiles + 2 * halo_tiles
    off = CONV_HALO - CONV_K // 2

    def chunk(cidx, carry):
        c0 = cidx * chunk_tiles
        for cb in range(D_CONV // LANES):
            lanes = slice(cb * LANES, (cb + 1) * LANES)
            win = ext_ref[pl.ds(c0, win_tiles), :, lanes].reshape(win_tiles * SUBLANES, LANES)
            acc = jnp.zeros((CONV_CHUNK, LANES), F32)
            for s in range(SUBLANES):
                taps = [k for k in range(CONV_K) if (k + off) % SUBLANES == s]
                if not taps:
                    continue
                sh = win if s == 0 else pltpu.roll(win, win_tiles * SUBLANES - s, axis=0)
                for k in taps:
                    a0 = k + off - s
                    acc = acc + sh[a0:a0 + CONV_CHUNK] * dww_ref[k:k + 1, lanes]
            conv_ref[pl.ds(c0, chunk_tiles), :, lanes] = acc.reshape(chunk_tiles, SUBLANES, LANES)
        return carry

    lax.fori_loop(0, tm // CONV_CHUNK, chunk, 0)

    a = conv_ref[...].reshape(tm, D_CONV) + dwb_ref[...]
    mu = jnp.mean(a, axis=-1, keepdims=True)
    d = a - mu
    var = jnp.mean(d * d, axis=-1, keepdims=True)
    y = d * lax.rsqrt(var + EPS) * lng_ref[...] + lnb_ref[...]
    y = y * _sigmoid(y)
    out = jnp.dot(y.astype(BF16), w_ref[...], preferred_element_type=F32) + b_ref[...]
    o_ref[...] = (out * gc_ref[...].astype(F32)).astype(BF16)


def _conformer_conv(z, dw_w, dw_b, ln_g, ln_b, w_pw2, b_pw2, *, seq, tm):
    t = z.shape[0]
    tiles_per_seq = seq // tm
    hb = tm // CONV_HALO
    n_hb = t // CONV_HALO

    def main(cb):
        return pl.BlockSpec((tm, D_CONV), lambda i: (i, cb))

    def before(cb):
        return pl.BlockSpec((CONV_HALO, D_CONV), lambda i: (jnp.maximum(i * hb - 1, 0), cb))

    def after(cb):
        return pl.BlockSpec((CONV_HALO, D_CONV),
                            lambda i: (jnp.minimum((i + 1) * hb, n_hb - 1), cb))

    def full(shape):
        return pl.BlockSpec(shape, lambda i: (0,) * len(shape))

    ext_tiles = (tm + 2 * CONV_HALO) // SUBLANES
    return pl.pallas_call(
        functools.partial(_conv_kernel, tm=tm, tiles_per_seq=tiles_per_seq),
        grid=(t // tm,),
        in_specs=[main(COL_U), before(COL_U), after(COL_U),
                  main(COL_UG), before(COL_UG), after(COL_UG), main(COL_GC),
                  full((CONV_K, D_CONV)), full((1, D_CONV)), full((1, D_CONV)), full((1, D_CONV)),
                  full((D_CONV, D_CONV)), full((1, D_CONV))],
        out_specs=pl.BlockSpec((tm, D_CONV), lambda i: (i, 0)),
        out_shape=jax.ShapeDtypeStruct((t, D_CONV), BF16),
        scratch_shapes=[pltpu.VMEM((ext_tiles, SUBLANES, D_CONV), F32),
                        pltpu.VMEM((tm // SUBLANES, SUBLANES, D_CONV), F32)],
        compiler_params=_params("arbitrary"),
        name="conformer_conv",
    )(z, z, z, z, z, z, z, dw_w, dw_b.reshape(1, D_CONV), ln_g.reshape(1, D_CONV),
      ln_b.reshape(1, D_CONV), w_pw2, b_pw2.reshape(1, D_CONV))


def _outproj_kernel(x_ref, a_ref, c_ref, gate_ref, w_ref, fg_ref, o_ref, cat_ref, *, final):
    cat_ref[:, :D_ATTN] = a_ref[...]
    cat_ref[:, D_ATTN:] = c_ref[...]
    out = jnp.dot(cat_ref[...], w_ref[...], preferred_element_type=F32)
    x = x_ref[...] + gate_ref[...] * out
    if final:
        x = x * lax.rsqrt(jnp.mean(x * x, axis=-1, keepdims=True) + EPS) * fg_ref[...]
    o_ref[...] = x


def _outproj(x, attn, conv, mod, w_out, final_g, *, rows_per_mod, mod_row0, final):
    t = x.shape[0]
    tm = 512
    tiles_per_mod = rows_per_mod // tm
    return pl.pallas_call(
        functools.partial(_outproj_kernel, final=final),
        grid=(t // tm,),
        in_specs=[
            pl.BlockSpec((tm, D_MODEL), lambda i: (i, 0)),
            pl.BlockSpec((tm, D_ATTN), lambda i: (i, 0)),
            pl.BlockSpec((tm, D_CONV), lambda i: (i, 0)),
            pl.BlockSpec((None, 1, D_MODEL),
                         lambda i: ((mod_row0 + i // tiles_per_mod) * 3 + 2, 0, 0)),
            pl.BlockSpec((D_MODEL, D_MODEL), lambda i: (0, 0)),
            pl.BlockSpec((1, D_MODEL), lambda i: (0, 0)),
        ],
        out_specs=pl.BlockSpec((tm, D_MODEL), lambda i: (i, 0)),
        out_shape=jax.ShapeDtypeStruct((t, D_MODEL), F32),
        scratch_shapes=[pltpu.VMEM((tm, D_MODEL), BF16)],
        compiler_params=_params("arbitrary"),
        name="outproj_final" if final else "outproj",
    )(x, attn, conv, mod, w_out, final_g.reshape(1, D_MODEL))


def kernel(x_prompt, x_sample, cache_k, cache_v, c, c_ctx, norm_g, w_ada, b_ada, w_in, b_in, rpb,
           dw_w, dw_b, cln_g, cln_b, w_pw2, b_pw2, w_out, final_norm_g):
    batch, seq, _ = x_prompt.shape
    dec_batch, dec_seq, _ = x_sample.shape
    assert dec_batch + 1 <= MOD_ROWS

    cvec = jnp.zeros((MOD_ROWS, D_MODEL), F32).at[:dec_batch].set(c).at[dec_batch].set(c_ctx)
    mod = _modulation(cvec, w_ada, b_ada).reshape(DEPTH, MOD_ROWS * 3, 1, D_MODEL)

    xp = x_prompt.reshape(batch * seq, D_MODEL)
    xs = x_sample.reshape(dec_batch * dec_seq, D_MODEL)
    caches = [jnp.zeros((batch, DEPTH, seq, D_ATTN), F32) for _ in range(2)]
    for l in range(DEPTH):
        final = l == DEPTH - 1
        w_in_l, w_out_l = w_in[l].astype(BF16), w_out[l].astype(BF16)
        conv_w = (dw_w[l], dw_b[l], cln_g[l], cln_b[l], w_pw2[l].astype(BF16), b_pw2[l])
        prompt_mod = dict(rows_per_mod=batch * seq, mod_row0=dec_batch)
        sample_mod = dict(rows_per_mod=dec_seq, mod_row0=0)

        zp, *caches = _inproj(xp, norm_g[l], mod[l], w_in_l, b_in[l], kv=(l, seq, caches),
                              **prompt_mod)
        zs, = _inproj(xs, norm_g[l], mod[l], w_in_l, b_in[l], **sample_mod)

        ap = _ctx_attention(zp, seq)
        bias = _nbr_bias_table(rpb[l], dec_seq // GRID_W)
        a_s = _nbr_attention(zs, cache_k, cache_v, bias, l, dec_seq)

        cp = _conformer_conv(zp, *conv_w, seq=seq, tm=seq)
        cs = _conformer_conv(zs, *conv_w, seq=dec_seq, tm=512)

        xp = _outproj(xp, ap, cp, mod[l], w_out_l, final_norm_g, final=final, **prompt_mod)
        xs = _outproj(xs, a_s, cs, mod[l], w_out_l, final_norm_g, final=final, **sample_mod)

    y_prompt = xp.reshape(batch, seq, D_MODEL)
    y_sample = xs.reshape(dec_batch, dec_seq, D_MODEL)
    new_k, new_v = (a.reshape(batch, DEPTH, seq, N_HEADS_A, HEAD_DIM) for a in caches)
    return (y_prompt, y_sample, new_k, new_v)
```

```python
import functools

import numpy as np
import jax
import jax.numpy as jnp
from jax import lax
from jax.experimental import pallas as pl
from jax.experimental.pallas import tpu as pltpu

D_MODEL = 2048
DEPTH = 2
GRID_W = 64
N_HEADS_A = 8
HEAD_DIM = 128
D_ATTN = N_HEADS_A * HEAD_DIM
D_CONV = D_MODEL - D_ATTN
D_IN = 4 * D_ATTN + 3 * D_CONV
WIN_R = 8
WIN_C = 16
CONV_K = 31
EPS = 1e-6

F32 = jnp.float32
BF16 = jnp.bfloat16

VMEM_LIMIT_BYTES = 56 * 1024 * 1024
LANES = 128
SUBLANES = 8

MOD_ROWS = 16
MASKED = -1e30
LOG2E = float(np.log2(np.e))
Q_SCALE = HEAD_DIM ** -0.5 * LOG2E

COL_Q, COL_K, COL_V, COL_GA, COL_U, COL_UG, COL_GC = range(7)

Q_ROWS = 4
K_ROWS = 12


def _sigmoid(x):
    return 0.5 * jnp.tanh(0.5 * x) + 0.5


def _params(*sem):
    return pltpu.CompilerParams(dimension_semantics=sem, vmem_limit_bytes=VMEM_LIMIT_BYTES)


def _mod_kernel(c_ref, w_ref, b_ref, o_ref):
    c = c_ref[...]
    s = c * jax.nn.sigmoid(c)
    o_ref[...] = jnp.dot(s.astype(BF16), w_ref[...].astype(BF16),
                         preferred_element_type=F32) + b_ref[...]


def _modulation(cvec, w_ada, b_ada):
    tn = 1536
    return pl.pallas_call(
        _mod_kernel,
        grid=(DEPTH, 3 * D_MODEL // tn),
        in_specs=[
            pl.BlockSpec((MOD_ROWS, D_MODEL), lambda l, j: (0, 0)),
            pl.BlockSpec((None, D_MODEL, tn), lambda l, j: (l, 0, j)),
            pl.BlockSpec((None, 1, tn), lambda l, j: (l, 0, j)),
        ],
        out_specs=pl.BlockSpec((None, MOD_ROWS, tn), lambda l, j: (l, 0, j)),
        out_shape=jax.ShapeDtypeStruct((DEPTH, MOD_ROWS, 3 * D_MODEL), F32),
        compiler_params=_params("arbitrary", "arbitrary"),
        name="modulation",
    )(cvec, w_ada, b_ada.reshape(DEPTH, 1, 3 * D_MODEL))


NORM_ROWS = 128

def _inproj_kernel(x_ref, g_ref, shift_ref, scl_ref, w_ref, b_ref, *rest, emit_kv):
    if emit_kv:
        _, _, z_ref, kc_ref, vc_ref, h_ref = rest
    else:
        z_ref, h_ref = rest
    j = pl.program_id(1)

    @pl.when(j == 0)
    def _():
        gain = g_ref[...] * (1.0 + scl_ref[...])
        shift = shift_ref[...]

        def norm_rows(r, carry):
            rows = pl.ds(pl.multiple_of(r * NORM_ROWS, NORM_ROWS), NORM_ROWS)
            x = x_ref[rows, :]
            inv = lax.rsqrt(jnp.mean(x * x, axis=-1, keepdims=True) + EPS)
            h_ref[rows, :] = (x_ref[rows, :] * inv * gain + shift).astype(BF16)
            return carry

        lax.fori_loop(0, x_ref.shape[0] // NORM_ROWS, norm_rows, 0)

    def project():
        return jnp.dot(h_ref[...], w_ref[...], preferred_element_type=F32) + b_ref[...]

    gate_col = (j == COL_GA) | (j == COL_UG) | (j == COL_GC)

    @pl.when(gate_col)
    def _():
        acc = project()
        sig = _sigmoid(acc)
        z_ref[...] = jnp.where(j == COL_UG, sig, acc * sig).astype(BF16)

    @pl.when(jnp.logical_not(gate_col))
    def _():
        acc = project()
        if emit_kv:
            @pl.when(j == COL_K)
            def _():
                kc_ref[...] = acc.reshape(kc_ref.shape)

            @pl.when(j == COL_V)
            def _():
                vc_ref[...] = acc.reshape(vc_ref.shape)
        z_ref[...] = (acc * jnp.where(j == COL_Q, Q_SCALE, 1.0)).astype(BF16)


def _inproj(x, norm_g, mod, w_in, b_in, *, rows_per_mod, mod_row0, kv=None):
    t = x.shape[0]
    tm, tn = 1024, 1024
    tiles_per_mod = rows_per_mod // tm

    def mod_idx(part):
        return lambda i, j: ((mod_row0 + i // tiles_per_mod) * 3 + part, 0, 0)

    args = [x, norm_g.reshape(1, D_MODEL), mod, mod, w_in, b_in.reshape(1, D_IN)]
    x_mode = dict(pipeline_mode=pl.Buffered(1)) if kv is not None else {}
    in_specs = [
        pl.BlockSpec((tm, D_MODEL), lambda i, j: (i, 0), **x_mode),
        pl.BlockSpec((1, D_MODEL), lambda i, j: (0, 0)),
        pl.BlockSpec((None, 1, D_MODEL), mod_idx(0)),
        pl.BlockSpec((None, 1, D_MODEL), mod_idx(1)),
        pl.BlockSpec((D_MODEL, tn), lambda i, j: (0, j)),
        pl.BlockSpec((1, tn), lambda i, j: (0, j)),
    ]
    out_shape = [jax.ShapeDtypeStruct((t, D_IN), BF16)]
    out_specs = [pl.BlockSpec((tm, tn), lambda i, j: (i, j))]
    aliases = {}
    if kv is not None:
        layer, seq, caches = kv
        assert tm % seq == 0
        cache_shape = jax.ShapeDtypeStruct((t // seq, DEPTH, seq, D_ATTN), F32)
        cache_spec = pl.BlockSpec((tm // seq, None, seq, D_ATTN), lambda i, j: (i, layer, 0, 0))
        out_shape += [cache_shape, cache_shape]
        out_specs += [cache_spec, cache_spec]
        aliases = {len(args): 1, len(args) + 1: 2}
        args += list(caches)
        in_specs += [pl.BlockSpec(memory_space=pl.ANY)] * 2
    return pl.pallas_call(
        functools.partial(_inproj_kernel, emit_kv=kv is not None),
        grid=(t // tm, D_IN // tn),
        in_specs=in_specs,
        out_specs=out_specs,
        out_shape=out_shape,
        input_output_aliases=aliases,
        scratch_shapes=[pltpu.VMEM((tm, D_MODEL), BF16)],
        compiler_params=_params("arbitrary", "arbitrary"),
        name="inproj_kv" if kv is not None else "inproj",
    )(*args)


def _dot_nt(a, b):
    return lax.dot_general(a, b, (((1,), (1,)), ((), ())), preferred_element_type=F32)


def _ctx_attn_kernel(q_ref, k_ref, v_ref, ga_ref, o_ref):
    for h in range(N_HEADS_A):
        cols = slice(h * HEAD_DIM, (h + 1) * HEAD_DIM)
        s = _dot_nt(q_ref[:, cols], k_ref[:, cols])
        p = jnp.exp2(s - jnp.max(s, axis=-1, keepdims=True))
        denom = jnp.sum(p, axis=-1, keepdims=True)
        o = jnp.dot(p.astype(BF16), v_ref[:, cols], preferred_element_type=F32)
        o_ref[:, cols] = (o / denom * ga_ref[:, cols].astype(F32)).astype(BF16)


def _ctx_attention(z, seq):
    t = z.shape[0]

    def col(cb):
        return pl.BlockSpec((seq, D_ATTN), lambda b: (b, cb))

    return pl.pallas_call(
        _ctx_attn_kernel,
        grid=(t // seq,),
        in_specs=[col(COL_Q), col(COL_K), col(COL_V), col(COL_GA)],
        out_specs=pl.BlockSpec((seq, D_ATTN), lambda b: (b, 0)),
        out_shape=jax.ShapeDtypeStruct((t, D_ATTN), BF16),
        compiler_params=_params("arbitrary"),
        name="ctx_attention",
    )(z, z, z, z)


def _nbr_bias_table(rpb, rows):
    n_groups = rows // Q_ROWS
    j = np.arange(GRID_W)[:, None]
    cc = np.arange(GRID_W)[None, :]
    cs = np.clip(j - WIN_C // 2, 0, GRID_W - WIN_C)
    valid_c = (cc >= cs) & (cc < cs + WIN_C)
    dc = cc - j + WIN_C - 1
    onehot = ((dc[None] == np.arange(2 * WIN_C - 1)[:, None, None]) & valid_c[None])
    toe = jnp.einsum('lhrd,djc->lhjrc', rpb, onehot.astype(np.float32),
                     precision=lax.Precision.HIGHEST)
    toe = jnp.where(valid_c[:, None, :], toe * LOG2E, MASKED).astype(F32)
    kinds = []
    for g in (0, 1, n_groups - 1):
        base = min(max(Q_ROWS * g - WIN_R // 2, 0), rows - K_ROWS)
        q_rows = []
        for a in range(Q_ROWS):
            i = Q_ROWS * g + a
            rs = min(max(i - WIN_R // 2, 0), rows - WIN_R)
            t0 = rs - base
            dr0 = rs - i + WIN_R - 1
            win = toe[:, :, :, dr0:dr0 + WIN_R, :].reshape(toe.shape[:3] + (WIN_R * GRID_W,))
            pad = ((0, 0), (0, 0), (0, 0), (t0 * GRID_W, (K_ROWS - WIN_R - t0) * GRID_W))
            q_rows.append(jnp.pad(win, pad, constant_values=MASKED))
        kinds.append(jnp.concatenate(q_rows, axis=-2))
    return jnp.stack(kinds, axis=2)


def _nbr_attn_kernel(q_ref, k_ref, v_ref, ga_ref, ck_ref, cv_ref, bias_ref, o_ref,
                     vaug_ref, p_ref, *, rows):
    n_groups = rows // Q_ROWS
    nq = Q_ROWS * GRID_W
    nk = K_ROWS * GRID_W
    seq = rows * GRID_W
    past = ck_ref.shape[0]

    vaug_ref[0:seq, 0:HEAD_DIM] = v_ref[...]
    vaug_ref[seq:seq + past, 0:HEAD_DIM] = cv_ref[...]
    vaug_ref[:, HEAD_DIM:] = jnp.ones((seq + past, HEAD_DIM), BF16)

    def slab_start(g):
        base = jnp.clip(Q_ROWS * g - WIN_R // 2, 0, rows - K_ROWS)
        return pl.multiple_of(base * GRID_W, GRID_W)

    def scores(g, slot):
        kind = jnp.where(g == 0, 0, jnp.where(g == n_groups - 1, 2, 1))
        q = q_ref[pl.ds(pl.multiple_of(g * nq, nq), nq), :]
        s_loc = _dot_nt(q, k_ref[pl.ds(slab_start(g), nk), :]) + bias_ref[kind]
        s_ctx = _dot_nt(q, ck_ref[...])
        m = jnp.maximum(jnp.max(s_loc, axis=-1, keepdims=True),
                        jnp.max(s_ctx, axis=-1, keepdims=True))
        p_ref[slot, :, 0:nk] = jnp.exp2(s_loc - m).astype(BF16)
        p_ref[slot, :, nk:] = jnp.exp2(s_ctx - m).astype(BF16)

    def values(g, slot):
        q0 = pl.multiple_of(g * nq, nq)
        o = (jnp.dot(p_ref[slot, :, 0:nk], vaug_ref[pl.ds(slab_start(g), nk), :],
                     preferred_element_type=F32)
             + jnp.dot(p_ref[slot, :, nk:], vaug_ref[seq:seq + past, :],
                       preferred_element_type=F32))
        gate = ga_ref[pl.ds(q0, nq), :].astype(F32)
        o_ref[pl.ds(q0, nq), :] = (o[:, :HEAD_DIM] / o[:, HEAD_DIM:] * gate).astype(BF16)

    def pair(k, score_slots, value_slots):
        scores(2 * k, score_slots[0])
        values(2 * k - 2, value_slots[0])
        scores(2 * k + 1, score_slots[1])
        values(2 * k - 1, value_slots[1])

    n_pairs = n_groups // 2
    scores(0, 0)
    scores(1, 1)

    def step(i, carry):
        pair(3 * i + 1, (2, 3), (0, 1))
        pair(3 * i + 2, (4, 5), (2, 3))
        pair(3 * i + 3, (0, 1), (4, 5))
        return carry

    lax.fori_loop(0, (n_pairs - 2) // 3, step, 0)
    pair(n_pairs - 1, (2, 3), (0, 1))
    values(n_groups - 2, 2)
    values(n_groups - 1, 3)


def _nbr_attention(z, cache_k, cache_v, bias, layer, seq):
    t = z.shape[0]
    rows = seq // GRID_W
    n_groups = rows // Q_ROWS
    assert n_groups % 2 == 0 and (n_groups // 2 - 2) % 3 == 0
    past = cache_k.shape[2]

    def col(cb):
        return pl.BlockSpec((seq, HEAD_DIM), lambda h, b: (b, cb * N_HEADS_A + h))

    cache_spec = pl.BlockSpec((None, None, past, HEAD_DIM), lambda h, b: (b, layer, 0, h))
    return pl.pallas_call(
        functools.partial(_nbr_attn_kernel, rows=rows),
        grid=(N_HEADS_A, t // seq),
        in_specs=[col(COL_Q), col(COL_K), col(COL_V), col(COL_GA), cache_spec, cache_spec,
                  pl.BlockSpec((None, 3, Q_ROWS * GRID_W, K_ROWS * GRID_W),
                               lambda h, b: (h, 0, 0, 0))],
        out_specs=pl.BlockSpec((seq, HEAD_DIM), lambda h, b: (b, h)),
        out_shape=jax.ShapeDtypeStruct((t, D_ATTN), BF16),
        scratch_shapes=[pltpu.VMEM((seq + past, 2 * HEAD_DIM), BF16),
                        pltpu.VMEM((6, Q_ROWS * GRID_W, K_ROWS * GRID_W + past), BF16)],
        compiler_params=_params("arbitrary", "arbitrary"),
        name="nbr_attention",
    )(z, z, z, z, cache_k, cache_v, bias)


CONV_HALO = 16
CONV_CHUNK = 64


def _conv_kernel(u_ref, up_ref, un_ref, s_ref, sp_ref, sn_ref, gc_ref, dww_ref, dwb_ref,
                 lng_ref, lnb_ref, w_ref, b_ref, o_ref, ext_ref, conv_ref, *, tm, tiles_per_seq):
    i = pl.program_id(0)
    first = (i % tiles_per_seq) == 0
    last = (i % tiles_per_seq) == tiles_per_seq - 1
    halo_tiles = CONV_HALO // SUBLANES
    n_tiles = tm // SUBLANES

    def glu(u, s):
        return u[...].astype(F32) * s[...].astype(F32)

    def tiles(a):
        return a.reshape(a.shape[0] // SUBLANES, SUBLANES, D_CONV)

    ext_ref[0:halo_tiles] = tiles(jnp.where(first, 0.0, glu(up_ref, sp_ref)))
    ext_ref[halo_tiles:halo_tiles + n_tiles] = tiles(glu(u_ref, s_ref))
    ext_ref[halo_tiles + n_tiles:] = tiles(jnp.where(last, 0.0, glu(un_ref, sn_ref)))

    chunk_tiles = CONV_CHUNK // SUBLANES
    win_tiles = chunk_tiles + 2 * halo_tiles
    off = CONV_HALO - CONV_K // 2

    def chunk(cidx, carry):
        c0 = cidx * chunk_tiles
        for cb in range(D_CONV // LANES):
            lanes = slice(cb * LANES, (cb + 1) * LANES)
            win = ext_ref[pl.ds(c0, win_tiles), :, lanes].reshape(win_tiles * SUBLANES, LANES)
            acc = jnp.zeros((CONV_CHUNK, LANES), F32)
            for s in range(SUBLANES):
                taps = [k for k in range(CONV_K) if (k + off) % SUBLANES == s]
                if not taps:
                    continue
                sh = win if s == 0 else pltpu.roll(win, win_tiles * SUBLANES - s, axis=0)
                for k in taps:
                    a0 = k + off - s
                    acc = acc + sh[a0:a0 + CONV_CHUNK] * dww_ref[k:k + 1, lanes]
            conv_ref[pl.ds(c0, chunk_tiles), :, lanes] = acc.reshape(chunk_tiles, SUBLANES, LANES)
        return carry

    lax.fori_loop(0, tm // CONV_CHUNK, chunk, 0)

    a = conv_ref[...].reshape(tm, D_CONV) + dwb_ref[...]
    mu = jnp.mean(a, axis=-1, keepdims=True)
    d = a - mu
    var = jnp.mean(d * d, axis=-1, keepdims=True)
    y = d * lax.rsqrt(var + EPS) * lng_ref[...] + lnb_ref[...]
    y = y * _sigmoid(y)
    out = jnp.dot(y.astype(BF16), w_ref[...], preferred_element_type=F32) + b_ref[...]
    o_ref[...] = (out * gc_ref[...].astype(F32)).astype(BF16)


def _conformer_conv(z, dw_w, dw_b, ln_g, ln_b, w_pw2, b_pw2, *, seq, tm):
    t = z.shape[0]
    tiles_per_seq = seq // tm
    hb = tm // CONV_HALO
    n_hb = t // CONV_HALO

    def main(cb):
        return pl.BlockSpec((tm, D_CONV), lambda i: (i, cb))

    def before(cb):
        return pl.BlockSpec((CONV_HALO, D_CONV), lambda i: (jnp.maximum(i * hb - 1, 0), cb))

    def after(cb):
        return pl.BlockSpec((CONV_HALO, D_CONV),
                            lambda i: (jnp.minimum((i + 1) * hb, n_hb - 1), cb))

    def full(shape):
        return pl.BlockSpec(shape, lambda i: (0,) * len(shape))

    ext_tiles = (tm + 2 * CONV_HALO) // SUBLANES
    return pl.pallas_call(
        functools.partial(_conv_kernel, tm=tm, tiles_per_seq=tiles_per_seq),
        grid=(t // tm,),
        in_specs=[main(COL_U), before(COL_U), after(COL_U),
                  main(COL_UG), before(COL_UG), after(COL_UG), main(COL_GC),
                  full((CONV_K, D_CONV)), full((1, D_CONV)), full((1, D_CONV)), full((1, D_CONV)),
                  full((D_CONV, D_CONV)), full((1, D_CONV))],
        out_specs=pl.BlockSpec((tm, D_CONV), lambda i: (i, 0)),
        out_shape=jax.ShapeDtypeStruct((t, D_CONV), BF16),
        scratch_shapes=[pltpu.VMEM((ext_tiles, SUBLANES, D_CONV), F32),
                        pltpu.VMEM((tm // SUBLANES, SUBLANES, D_CONV), F32)],
        compiler_params=_params("arbitrary"),
        name="conformer_conv",
    )(z, z, z, z, z, z, z, dw_w, dw_b.reshape(1, D_CONV), ln_g.reshape(1, D_CONV),
      ln_b.reshape(1, D_CONV), w_pw2, b_pw2.reshape(1, D_CONV))


def _outproj_kernel(x_ref, a_ref, c_ref, gate_ref, w_ref, fg_ref, o_ref, cat_ref, *, final):
    cat_ref[:, :D_ATTN] = a_ref[...]
    cat_ref[:, D_ATTN:] = c_ref[...]
    out = jnp.dot(cat_ref[...], w_ref[...], preferred_element_type=F32)
    x = x_ref[...] + gate_ref[...] * out
    if final:
        x = x * lax.rsqrt(jnp.mean(x * x, axis=-1, keepdims=True) + EPS) * fg_ref[...]
    o_ref[...] = x


def _outproj(x, attn, conv, mod, w_out, final_g, *, rows_per_mod, mod_row0, final):
    t = x.shape[0]
    tm = 512
    tiles_per_mod = rows_per_mod // tm
    return pl.pallas_call(
        functools.partial(_outproj_kernel, final=final),
        grid=(t // tm,),
        in_specs=[
            pl.BlockSpec((tm, D_MODEL), lambda i: (i, 0)),
            pl.BlockSpec((tm, D_ATTN), lambda i: (i, 0)),
            pl.BlockSpec((tm, D_CONV), lambda i: (i, 0)),
            pl.BlockSpec((None, 1, D_MODEL),
                         lambda i: ((mod_row0 + i // tiles_per_mod) * 3 + 2, 0, 0)),
            pl.BlockSpec((D_MODEL, D_MODEL), lambda i: (0, 0)),
            pl.BlockSpec((1, D_MODEL), lambda i: (0, 0)),
        ],
        out_specs=pl.BlockSpec((tm, D_MODEL), lambda i: (i, 0)),
        out_shape=jax.ShapeDtypeStruct((t, D_MODEL), F32),
        scratch_shapes=[pltpu.VMEM((tm, D_MODEL), BF16)],
        compiler_params=_params("arbitrary"),
        name="outproj_final" if final else "outproj",
    )(x, attn, conv, mod, w_out, final_g.reshape(1, D_MODEL))


def kernel(x_prompt, x_sample, cache_k, cache_v, c, c_ctx, norm_g, w_ada, b_ada, w_in, b_in, rpb,
           dw_w, dw_b, cln_g, cln_b, w_pw2, b_pw2, w_out, final_norm_g):
    batch, seq, _ = x_prompt.shape
    dec_batch, dec_seq, _ = x_sample.shape
    assert dec_batch + 1 <= MOD_ROWS

    cvec = jnp.zeros((MOD_ROWS, D_MODEL), F32).at[:dec_batch].set(c).at[dec_batch].set(c_ctx)
    mod = _modulation(cvec, w_ada, b_ada).reshape(DEPTH, MOD_ROWS * 3, 1, D_MODEL)

    xp = x_prompt.reshape(batch * seq, D_MODEL)
    xs = x_sample.reshape(dec_batch * dec_seq, D_MODEL)
    caches = [jnp.zeros((batch, DEPTH, seq, D_ATTN), F32) for _ in range(2)]
    bias = _nbr_bias_table(rpb, dec_seq // GRID_W)
    past = cache_k.shape[2]
    ctx_k = cache_k.astype(BF16).reshape(dec_batch, DEPTH, past, D_ATTN)
    ctx_v = cache_v.astype(BF16).reshape(dec_batch, DEPTH, past, D_ATTN)
    for l in range(DEPTH):
        final = l == DEPTH - 1
        w_in_l, w_out_l = w_in[l].astype(BF16), w_out[l].astype(BF16)
        conv_w = (dw_w[l], dw_b[l], cln_g[l], cln_b[l], w_pw2[l].astype(BF16), b_pw2[l])
        prompt_mod = dict(rows_per_mod=batch * seq, mod_row0=dec_batch)
        sample_mod = dict(rows_per_mod=dec_seq, mod_row0=0)

        zp, *caches = _inproj(xp, norm_g[l], mod[l], w_in_l, b_in[l], kv=(l, seq, caches),
                              **prompt_mod)
        zs, = _inproj(xs, norm_g[l], mod[l], w_in_l, b_in[l], **sample_mod)

        ap = _ctx_attention(zp, seq)
        a_s = _nbr_attention(zs, ctx_k, ctx_v, bias[l], l, dec_seq)

        cp = _conformer_conv(zp, *conv_w, seq=seq, tm=seq)
        cs = _conformer_conv(zs, *conv_w, seq=dec_seq, tm=512)

        xp = _outproj(xp, ap, cp, mod[l], w_out_l, final_norm_g, final=final, **prompt_mod)
        xs = _outproj(xs, a_s, cs, mod[l], w_out_l, final_norm_g, final=final, **sample_mod)

    y_prompt = xp.reshape(batch, seq, D_MODEL)
    y_sample = xs.reshape(dec_batch, dec_seq, D_MODEL)
    new_k, new_v = (a.reshape(batch, DEPTH, seq, N_HEADS_A, HEAD_DIM) for a in caches)
    return (y_prompt, y_sample, new_k, new_v)
```

```python
import functools

import numpy as np
import jax
import jax.numpy as jnp
from jax import lax
from jax.experimental import pallas as pl
from jax.experimental.pallas import tpu as pltpu

D_MODEL = 2048
DEPTH = 2
GRID_W = 64
N_HEADS_A = 8
HEAD_DIM = 128
D_ATTN = N_HEADS_A * HEAD_DIM
D_CONV = D_MODEL - D_ATTN
D_IN = 4 * D_ATTN + 3 * D_CONV
WIN_R = 8
WIN_C = 16
CONV_K = 31
EPS = 1e-6

F32 = jnp.float32
BF16 = jnp.bfloat16

VMEM_LIMIT_BYTES = 56 * 1024 * 1024
LANES = 128
SUBLANES = 8

MOD_ROWS = 16
MASKED = -1e30
LOG2E = float(np.log2(np.e))
Q_SCALE = HEAD_DIM ** -0.5 * LOG2E

COL_Q, COL_K, COL_V, COL_GA, COL_U, COL_UG, COL_GC = range(7)

Q_ROWS = 4
K_ROWS = 12


def _sigmoid(x):
    return 0.5 * jnp.tanh(0.5 * x) + 0.5


def _params(*sem):
    return pltpu.CompilerParams(dimension_semantics=sem, vmem_limit_bytes=VMEM_LIMIT_BYTES)


def _mod_kernel(c_ref, w_ref, b_ref, o_ref):
    c = c_ref[...]
    s = c * jax.nn.sigmoid(c)
    o_ref[...] = jnp.dot(s.astype(BF16), w_ref[...].astype(BF16),
                         preferred_element_type=F32) + b_ref[...]


def _modulation(cvec, w_ada, b_ada):
    tn = 1536
    return pl.pallas_call(
        _mod_kernel,
        grid=(DEPTH, 3 * D_MODEL // tn),
        in_specs=[
            pl.BlockSpec((MOD_ROWS, D_MODEL), lambda l, j: (0, 0)),
            pl.BlockSpec((None, D_MODEL, tn), lambda l, j: (l, 0, j)),
            pl.BlockSpec((None, 1, tn), lambda l, j: (l, 0, j)),
        ],
        out_specs=pl.BlockSpec((None, MOD_ROWS, tn), lambda l, j: (l, 0, j)),
        out_shape=jax.ShapeDtypeStruct((DEPTH, MOD_ROWS, 3 * D_MODEL), F32),
        compiler_params=_params("arbitrary", "arbitrary"),
        name="modulation",
    )(cvec, w_ada, b_ada.reshape(DEPTH, 1, 3 * D_MODEL))


NORM_ROWS = 128

def _inproj_kernel(x_ref, g_ref, shift_ref, scl_ref, w_ref, b_ref, *rest, emit_kv):
    if emit_kv:
        _, _, z_ref, kc_ref, vc_ref, h_ref = rest
    else:
        z_ref, h_ref = rest
    j = pl.program_id(1)

    @pl.when(j == 0)
    def _():
        gain = g_ref[...] * (1.0 + scl_ref[...])
        shift = shift_ref[...]

        def norm_rows(r, carry):
            rows = pl.ds(pl.multiple_of(r * NORM_ROWS, NORM_ROWS), NORM_ROWS)
            x = x_ref[rows, :]
            inv = lax.rsqrt(jnp.mean(x * x, axis=-1, keepdims=True) + EPS)
            h_ref[rows, :] = (x_ref[rows, :] * inv * gain + shift).astype(BF16)
            return carry

        lax.fori_loop(0, x_ref.shape[0] // NORM_ROWS, norm_rows, 0)

    def project():
        return jnp.dot(h_ref[...], w_ref[...], preferred_element_type=F32) + b_ref[...]

    gate_col = (j == COL_GA) | (j == COL_UG) | (j == COL_GC)

    @pl.when(gate_col)
    def _():
        acc = project()
        sig = _sigmoid(acc)
        z_ref[...] = jnp.where(j == COL_UG, sig, acc * sig).astype(BF16)

    @pl.when(jnp.logical_not(gate_col))
    def _():
        acc = project()
        if emit_kv:
            @pl.when(j == COL_K)
            def _():
                kc_ref[...] = acc.reshape(kc_ref.shape)

            @pl.when(j == COL_V)
            def _():
                vc_ref[...] = acc.reshape(vc_ref.shape)
        z_ref[...] = (acc * jnp.where(j == COL_Q, Q_SCALE, 1.0)).astype(BF16)


def _inproj(x, norm_g, mod, w_in, b_in, *, rows_per_mod, mod_row0, kv=None):
    t = x.shape[0]
    tm, tn = 1024, 1024
    tiles_per_mod = rows_per_mod // tm

    def mod_idx(part):
        return lambda i, j: ((mod_row0 + i // tiles_per_mod) * 3 + part, 0, 0)

    args = [x, norm_g.reshape(1, D_MODEL), mod, mod, w_in, b_in.reshape(1, D_IN)]
    x_mode = dict(pipeline_mode=pl.Buffered(1)) if kv is not None else {}
    in_specs = [
        pl.BlockSpec((tm, D_MODEL), lambda i, j: (i, 0), **x_mode),
        pl.BlockSpec((1, D_MODEL), lambda i, j: (0, 0)),
        pl.BlockSpec((None, 1, D_MODEL), mod_idx(0)),
        pl.BlockSpec((None, 1, D_MODEL), mod_idx(1)),
        pl.BlockSpec((D_MODEL, tn), lambda i, j: (0, j)),
        pl.BlockSpec((1, tn), lambda i, j: (0, j)),
    ]
    out_shape = [jax.ShapeDtypeStruct((t, D_IN), BF16)]
    out_specs = [pl.BlockSpec((tm, tn), lambda i, j: (i, j))]
    aliases = {}
    if kv is not None:
        layer, seq, caches = kv
        assert tm % seq == 0
        cache_shape = jax.ShapeDtypeStruct((t // seq, DEPTH, seq, D_ATTN), F32)
        cache_spec = pl.BlockSpec((tm // seq, None, seq, D_ATTN), lambda i, j: (i, layer, 0, 0))
        out_shape += [cache_shape, cache_shape]
        out_specs += [cache_spec, cache_spec]
        aliases = {len(args): 1, len(args) + 1: 2}
        args += list(caches)
        in_specs += [pl.BlockSpec(memory_space=pl.ANY)] * 2
    return pl.pallas_call(
        functools.partial(_inproj_kernel, emit_kv=kv is not None),
        grid=(t // tm, D_IN // tn),
        in_specs=in_specs,
        out_specs=out_specs,
        out_shape=out_shape,
        input_output_aliases=aliases,
        scratch_shapes=[pltpu.VMEM((tm, D_MODEL), BF16)],
        compiler_params=_params("arbitrary", "arbitrary"),
        name="inproj_kv" if kv is not None else "inproj",
    )(*args)


def _dot_nt(a, b):
    return lax.dot_general(a, b, (((1,), (1,)), ((), ())), preferred_element_type=F32)


def _ctx_attn_kernel(q_ref, k_ref, v_ref, ga_ref, o_ref):
    for h in range(N_HEADS_A):
        cols = slice(h * HEAD_DIM, (h + 1) * HEAD_DIM)
        s = _dot_nt(q_ref[:, cols], k_ref[:, cols])
        p = jnp.exp2(s - jnp.max(s, axis=-1, keepdims=True))
        denom = jnp.sum(p, axis=-1, keepdims=True)
        o = jnp.dot(p.astype(BF16), v_ref[:, cols], preferred_element_type=F32)
        o_ref[:, cols] = (o / denom * ga_ref[:, cols].astype(F32)).astype(BF16)


def _ctx_attention(z, seq):
    t = z.shape[0]

    def col(cb):
        return pl.BlockSpec((seq, D_ATTN), lambda b: (b, cb))

    return pl.pallas_call(
        _ctx_attn_kernel,
        grid=(t // seq,),
        in_specs=[col(COL_Q), col(COL_K), col(COL_V), col(COL_GA)],
        out_specs=pl.BlockSpec((seq, D_ATTN), lambda b: (b, 0)),
        out_shape=jax.ShapeDtypeStruct((t, D_ATTN), BF16),
        compiler_params=_params("arbitrary"),
        name="ctx_attention",
    )(z, z, z, z)


def _bias_kernel(toe_ref, o_ref, *, rows):
    n_groups = rows // Q_ROWS
    masked = jnp.full((GRID_W, GRID_W), MASKED, F32)
    for kind, g in enumerate((0, 1, n_groups - 1)):
        base = min(max(Q_ROWS * g - WIN_R // 2, 0), rows - K_ROWS)
        for a in range(Q_ROWS):
            i = Q_ROWS * g + a
            rs = min(max(i - WIN_R // 2, 0), rows - WIN_R)
            blocks = [toe_ref[base + t - i + WIN_R - 1] if rs <= base + t < rs + WIN_R else masked
                      for t in range(K_ROWS)]
            o_ref[kind, a * GRID_W:(a + 1) * GRID_W, :] = jnp.concatenate(blocks, axis=-1)


def _nbr_bias_table(rpb, rows):
    n_layers, n_heads, n_dr, _ = rpb.shape
    j = np.arange(GRID_W)[:, None]
    cc = np.arange(GRID_W)[None, :]
    cs = np.clip(j - WIN_C // 2, 0, GRID_W - WIN_C)
    valid_c = (cc >= cs) & (cc < cs + WIN_C)
    dc = cc - j + WIN_C - 1
    onehot = ((dc[None] == np.arange(2 * WIN_C - 1)[:, None, None]) & valid_c[None])
    toe = jnp.einsum('lhrd,djc->lhrjc', rpb, onehot.astype(np.float32),
                     precision=lax.Precision.HIGHEST)
    toe = jnp.where(valid_c, toe * LOG2E, MASKED).astype(F32)
    nq, nk = Q_ROWS * GRID_W, K_ROWS * GRID_W
    return pl.pallas_call(
        functools.partial(_bias_kernel, rows=rows),
        grid=(n_layers, n_heads),
        in_specs=[pl.BlockSpec((None, None, n_dr, GRID_W, GRID_W), lambda l, h: (l, h, 0, 0, 0))],
        out_specs=pl.BlockSpec((None, None, 3, nq, nk), lambda l, h: (l, h, 0, 0, 0)),
        out_shape=jax.ShapeDtypeStruct((n_layers, n_heads, 3, nq, nk), F32),
        compiler_params=_params("arbitrary", "arbitrary"),
        name="nbr_bias_table",
    )(toe)


def _nbr_attn_kernel(q_ref, k_ref, v_ref, ga_ref, ck_ref, cv_ref, bias_ref, o_ref,
                     kc_ref, vaug_ref, p_ref, *, rows):
    n_groups = rows // Q_ROWS
    nq = Q_ROWS * GRID_W
    nk = K_ROWS * GRID_W
    seq = rows * GRID_W
    past = ck_ref.shape[0]

    kc_ref[...] = ck_ref[...].astype(BF16)
    vaug_ref[0:seq, 0:HEAD_DIM] = v_ref[...]
    vaug_ref[seq:seq + past, 0:HEAD_DIM] = cv_ref[...].astype(BF16)
    vaug_ref[:, HEAD_DIM:] = jnp.ones((seq + past, HEAD_DIM), BF16)

    def slab_start(g):
        base = jnp.clip(Q_ROWS * g - WIN_R // 2, 0, rows - K_ROWS)
        return pl.multiple_of(base * GRID_W, GRID_W)

    def scores(g, slot):
        kind = jnp.where(g == 0, 0, jnp.where(g == n_groups - 1, 2, 1))
        q = q_ref[pl.ds(pl.multiple_of(g * nq, nq), nq), :]
        s_loc = _dot_nt(q, k_ref[pl.ds(slab_start(g), nk), :]) + bias_ref[kind]
        s_ctx = _dot_nt(q, kc_ref[...])
        m = jnp.maximum(jnp.max(s_loc, axis=-1, keepdims=True),
                        jnp.max(s_ctx, axis=-1, keepdims=True))
        p_ref[slot, :, 0:nk] = jnp.exp2(s_loc - m).astype(BF16)
        p_ref[slot, :, nk:] = jnp.exp2(s_ctx - m).astype(BF16)

    def values(g, slot):
        q0 = pl.multiple_of(g * nq, nq)
        o = (jnp.dot(p_ref[slot, :, 0:nk], vaug_ref[pl.ds(slab_start(g), nk), :],
                     preferred_element_type=F32)
             + jnp.dot(p_ref[slot, :, nk:], vaug_ref[seq:seq + past, :],
                       preferred_element_type=F32))
        gate = ga_ref[pl.ds(q0, nq), :].astype(F32)
        o_ref[pl.ds(q0, nq), :] = (o[:, :HEAD_DIM] / o[:, HEAD_DIM:] * gate).astype(BF16)

    def pair(k, score_slots, value_slots):
        scores(2 * k, score_slots[0])
        values(2 * k - 2, value_slots[0])
        scores(2 * k + 1, score_slots[1])
        values(2 * k - 1, value_slots[1])

    n_pairs = n_groups // 2
    scores(0, 0)
    scores(1, 1)

    def step(i, carry):
        pair(3 * i + 1, (2, 3), (0, 1))
        pair(3 * i + 2, (4, 5), (2, 3))
        pair(3 * i + 3, (0, 1), (4, 5))
        return carry

    lax.fori_loop(0, (n_pairs - 2) // 3, step, 0)
    pair(n_pairs - 1, (2, 3), (0, 1))
    values(n_groups - 2, 2)
    values(n_groups - 1, 3)


def _nbr_attention(z, cache_k, cache_v, bias, layer, seq):
    t = z.shape[0]
    rows = seq // GRID_W
    n_groups = rows // Q_ROWS
    assert n_groups % 2 == 0 and (n_groups // 2 - 2) % 3 == 0
    past = cache_k.shape[2]

    def col(cb):
        return pl.BlockSpec((seq, HEAD_DIM), lambda h, b: (b, cb * N_HEADS_A + h))

    cache_spec = pl.BlockSpec((None, None, past, HEAD_DIM), lambda h, b: (b, layer, 0, h))
    return pl.pallas_call(
        functools.partial(_nbr_attn_kernel, rows=rows),
        grid=(N_HEADS_A, t // seq),
        in_specs=[col(COL_Q), col(COL_K), col(COL_V), col(COL_GA), cache_spec, cache_spec,
                  pl.BlockSpec((None, 3, Q_ROWS * GRID_W, K_ROWS * GRID_W),
                               lambda h, b: (h, 0, 0, 0))],
        out_specs=pl.BlockSpec((seq, HEAD_DIM), lambda h, b: (b, h)),
        out_shape=jax.ShapeDtypeStruct((t, D_ATTN), BF16),
        scratch_shapes=[pltpu.VMEM((past, HEAD_DIM), BF16),
                        pltpu.VMEM((seq + past, 2 * HEAD_DIM), BF16),
                        pltpu.VMEM((6, Q_ROWS * GRID_W, K_ROWS * GRID_W + past), BF16)],
        compiler_params=_params("arbitrary", "arbitrary"),
        name="nbr_attention",
    )(z, z, z, z, cache_k, cache_v, bias)


CONV_HALO = 16
CONV_CHUNK = 64


def _conv_kernel(u_ref, up_ref, un_ref, s_ref, sp_ref, sn_ref, gc_ref, dww_ref, dwb_ref,
                 lng_ref, lnb_ref, w_ref, b_ref, o_ref, ext_ref, conv_ref, *, tm, tiles_per_seq):
    i = pl.program_id(0)
    first = (i % tiles_per_seq) == 0
    last = (i % tiles_per_seq) == tiles_per_seq - 1
    halo_tiles = CONV_HALO // SUBLANES
    n_tiles = tm // SUBLANES

    def glu(u, s):
        return u[...].astype(F32) * s[...].astype(F32)

    def tiles(a):
        return a.reshape(a.shape[0] // SUBLANES, SUBLANES, D_CONV)

    ext_ref[0:halo_tiles] = tiles(jnp.where(first, 0.0, glu(up_ref, sp_ref)))
    ext_ref[halo_tiles:halo_tiles + n_tiles] = tiles(glu(u_ref, s_ref))
    ext_ref[halo_tiles + n_tiles:] = tiles(jnp.where(last, 0.0, glu(un_ref, sn_ref)))

    chunk_tiles = CONV_CHUNK // SUBLANES
    win_tiles = chunk_tiles + 2 * halo_tiles
    off = CONV_HALO - CONV_K // 2

    def chunk(cidx, carry):
        c0 = cidx * chunk_tiles
        for cb in range(D_CONV // LANES):
            lanes = slice(cb * LANES, (cb + 1) * LANES)
            win = ext_ref[pl.ds(c0, win_tiles), :, lanes].reshape(win_tiles * SUBLANES, LANES)
            acc = jnp.zeros((CONV_CHUNK, LANES), F32)
            for s in range(SUBLANES):
                taps = [k for k in range(CONV_K) if (k + off) % SUBLANES == s]
                if not taps:
                    continue
                sh = win if s == 0 else pltpu.roll(win, win_tiles * SUBLANES - s, axis=0)
                for k in taps:
                    a0 = k + off - s
                    acc = acc + sh[a0:a0 + CONV_CHUNK] * dww_ref[k:k + 1, lanes]
            conv_ref[pl.ds(c0, chunk_tiles), :, lanes] = acc.reshape(chunk_tiles, SUBLANES, LANES)
        return carry

    lax.fori_loop(0, tm // CONV_CHUNK, chunk, 0)

    a = conv_ref[...].reshape(tm, D_CONV) + dwb_ref[...]
    mu = jnp.mean(a, axis=-1, keepdims=True)
    d = a - mu
    var = jnp.mean(d * d, axis=-1, keepdims=True)
    y = d * lax.rsqrt(var + EPS) * lng_ref[...] + lnb_ref[...]
    y = y * _sigmoid(y)
    out = jnp.dot(y.astype(BF16), w_ref[...], preferred_element_type=F32) + b_ref[...]
    o_ref[...] = (out * gc_ref[...].astype(F32)).astype(BF16)


def _conformer_conv(z, dw_w, dw_b, ln_g, ln_b, w_pw2, b_pw2, *, seq, tm):
    t = z.shape[0]
    tiles_per_seq = seq // tm
    hb = tm // CONV_HALO
    n_hb = t // CONV_HALO

    def main(cb):
        return pl.BlockSpec((tm, D_CONV), lambda i: (i, cb))

    def before(cb):
        return pl.BlockSpec((CONV_HALO, D_CONV), lambda i: (jnp.maximum(i * hb - 1, 0), cb))

    def after(cb):
        return pl.BlockSpec((CONV_HALO, D_CONV),
                            lambda i: (jnp.minimum((i + 1) * hb, n_hb - 1), cb))

    def full(shape):
        return pl.BlockSpec(shape, lambda i: (0,) * len(shape))

    ext_tiles = (tm + 2 * CONV_HALO) // SUBLANES
    return pl.pallas_call(
        functools.partial(_conv_kernel, tm=tm, tiles_per_seq=tiles_per_seq),
        grid=(t // tm,),
        in_specs=[main(COL_U), before(COL_U), after(COL_U),
                  main(COL_UG), before(COL_UG), after(COL_UG), main(COL_GC),
                  full((CONV_K, D_CONV)), full((1, D_CONV)), full((1, D_CONV)), full((1, D_CONV)),
                  full((D_CONV, D_CONV)), full((1, D_CONV))],
        out_specs=pl.BlockSpec((tm, D_CONV), lambda i: (i, 0)),
        out_shape=jax.ShapeDtypeStruct((t, D_CONV), BF16),
        scratch_shapes=[pltpu.VMEM((ext_tiles, SUBLANES, D_CONV), F32),
                        pltpu.VMEM((tm // SUBLANES, SUBLANES, D_CONV), F32)],
        compiler_params=_params("arbitrary"),
        name="conformer_conv",
    )(z, z, z, z, z, z, z, dw_w, dw_b.reshape(1, D_CONV), ln_g.reshape(1, D_CONV),
      ln_b.reshape(1, D_CONV), w_pw2, b_pw2.reshape(1, D_CONV))


def _outproj_kernel(x_ref, a_ref, c_ref, gate_ref, w_ref, fg_ref, o_ref, cat_ref, *, final):
    cat_ref[:, :D_ATTN] = a_ref[...]
    cat_ref[:, D_ATTN:] = c_ref[...]
    out = jnp.dot(cat_ref[...], w_ref[...], preferred_element_type=F32)
    x = x_ref[...] + gate_ref[...] * out
    if final:
        x = x * lax.rsqrt(jnp.mean(x * x, axis=-1, keepdims=True) + EPS) * fg_ref[...]
    o_ref[...] = x


def _outproj(x, attn, conv, mod, w_out, final_g, *, rows_per_mod, mod_row0, final):
    t = x.shape[0]
    tm = 512
    tiles_per_mod = rows_per_mod // tm
    return pl.pallas_call(
        functools.partial(_outproj_kernel, final=final),
        grid=(t // tm,),
        in_specs=[
            pl.BlockSpec((tm, D_MODEL), lambda i: (i, 0)),
            pl.BlockSpec((tm, D_ATTN), lambda i: (i, 0)),
            pl.BlockSpec((tm, D_CONV), lambda i: (i, 0)),
            pl.BlockSpec((None, 1, D_MODEL),
                         lambda i: ((mod_row0 + i // tiles_per_mod) * 3 + 2, 0, 0)),
            pl.BlockSpec((D_MODEL, D_MODEL), lambda i: (0, 0)),
            pl.BlockSpec((1, D_MODEL), lambda i: (0, 0)),
        ],
        out_specs=pl.BlockSpec((tm, D_MODEL), lambda i: (i, 0)),
        out_shape=jax.ShapeDtypeStruct((t, D_MODEL), F32),
        scratch_shapes=[pltpu.VMEM((tm, D_MODEL), BF16)],
        compiler_params=_params("arbitrary"),
        name="outproj_final" if final else "outproj",
    )(x, attn, conv, mod, w_out, final_g.reshape(1, D_MODEL))


def kernel(x_prompt, x_sample, cache_k, cache_v, c, c_ctx, norm_g, w_ada, b_ada, w_in, b_in, rpb,
           dw_w, dw_b, cln_g, cln_b, w_pw2, b_pw2, w_out, final_norm_g):
    batch, seq, _ = x_prompt.shape
    dec_batch, dec_seq, _ = x_sample.shape
    assert dec_batch + 1 <= MOD_ROWS

    cvec = jnp.zeros((MOD_ROWS, D_MODEL), F32).at[:dec_batch].set(c).at[dec_batch].set(c_ctx)
    mod = _modulation(cvec, w_ada, b_ada).reshape(DEPTH, MOD_ROWS * 3, 1, D_MODEL)

    xp = x_prompt.reshape(batch * seq, D_MODEL)
    xs = x_sample.reshape(dec_batch * dec_seq, D_MODEL)
    caches = [jnp.zeros((batch, DEPTH, seq, D_ATTN), F32) for _ in range(2)]
    bias = _nbr_bias_table(rpb, dec_seq // GRID_W)
    past = cache_k.shape[2]
    ctx_k = cache_k.reshape(dec_batch, DEPTH, past, D_ATTN)
    ctx_v = cache_v.reshape(dec_batch, DEPTH, past, D_ATTN)
    for l in range(DEPTH):
        final = l == DEPTH - 1
        w_in_l, w_out_l = w_in[l].astype(BF16), w_out[l].astype(BF16)
        conv_w = (dw_w[l], dw_b[l], cln_g[l], cln_b[l], w_pw2[l].astype(BF16), b_pw2[l])
        prompt_mod = dict(rows_per_mod=batch * seq, mod_row0=dec_batch)
        sample_mod = dict(rows_per_mod=dec_seq, mod_row0=0)

        zp, *caches = _inproj(xp, norm_g[l], mod[l], w_in_l, b_in[l], kv=(l, seq, caches),
                              **prompt_mod)
        zs, = _inproj(xs, norm_g[l], mod[l], w_in_l, b_in[l], **sample_mod)

        ap = _ctx_attention(zp, seq)
        a_s = _nbr_attention(zs, ctx_k, ctx_v, bias[l], l, dec_seq)

        cp = _conformer_conv(zp, *conv_w, seq=seq, tm=seq)
        cs = _conformer_conv(zs, *conv_w, seq=dec_seq, tm=512)

        xp = _outproj(xp, ap, cp, mod[l], w_out_l, final_norm_g, final=final, **prompt_mod)
        xs = _outproj(xs, a_s, cs, mod[l], w_out_l, final_norm_g, final=final, **sample_mod)

    y_prompt = xp.reshape(batch, seq, D_MODEL)
    y_sample = xs.reshape(dec_batch, dec_seq, D_MODEL)
    new_k, new_v = (a.reshape(batch, DEPTH, seq, N_HEADS_A, HEAD_DIM) for a in caches)
    return (y_prompt, y_sample, new_k, new_v)
```

```python
import functools

import numpy as np
import jax
import jax.numpy as jnp
from jax import lax
from jax.experimental import pallas as pl
from jax.experimental.pallas import tpu as pltpu

D_MODEL = 2048
DEPTH = 2
GRID_W = 64
N_HEADS_A = 8
HEAD_DIM = 128
D_ATTN = N_HEADS_A * HEAD_DIM
D_CONV = D_MODEL - D_ATTN
D_IN = 4 * D_ATTN + 3 * D_CONV
WIN_R = 8
WIN_C = 16
CONV_K = 31
EPS = 1e-6

F32 = jnp.float32
BF16 = jnp.bfloat16

VMEM_LIMIT_BYTES = 56 * 1024 * 1024
LANES = 128
SUBLANES = 8

MOD_ROWS = 16
MASKED = -1e30
LOG2E = float(np.log2(np.e))
Q_SCALE = HEAD_DIM ** -0.5 * LOG2E

COL_Q, COL_K, COL_V, COL_GA, COL_U, COL_UG, COL_GC = range(7)

Q_ROWS = 4
K_ROWS = 12


def _sigmoid(x):
    return 0.5 * jnp.tanh(0.5 * x) + 0.5


def _params(*sem):
    return pltpu.CompilerParams(dimension_semantics=sem, vmem_limit_bytes=VMEM_LIMIT_BYTES)


def _mod_kernel(c_ref, w_ref, b_ref, o_ref):
    c = c_ref[...]
    s = c * jax.nn.sigmoid(c)
    o_ref[...] = jnp.dot(s.astype(BF16), w_ref[...].astype(BF16),
                         preferred_element_type=F32) + b_ref[...]


def _modulation(cvec, w_ada, b_ada):
    tn = 1536
    return pl.pallas_call(
        _mod_kernel,
        grid=(DEPTH, 3 * D_MODEL // tn),
        in_specs=[
            pl.BlockSpec((MOD_ROWS, D_MODEL), lambda l, j: (0, 0)),
            pl.BlockSpec((None, D_MODEL, tn), lambda l, j: (l, 0, j)),
            pl.BlockSpec((None, 1, tn), lambda l, j: (l, 0, j)),
        ],
        out_specs=pl.BlockSpec((None, MOD_ROWS, tn), lambda l, j: (l, 0, j)),
        out_shape=jax.ShapeDtypeStruct((DEPTH, MOD_ROWS, 3 * D_MODEL), F32),
        compiler_params=_params("arbitrary", "arbitrary"),
        name="modulation",
    )(cvec, w_ada, b_ada.reshape(DEPTH, 1, 3 * D_MODEL))


NORM_ROWS = 128

def _inproj_kernel(x_ref, g_ref, shift_ref, scl_ref, w_ref, b_ref, *rest, emit_kv):
    if emit_kv:
        _, _, z_ref, kc_ref, vc_ref, h_ref = rest
    else:
        z_ref, h_ref = rest
    j = pl.program_id(1)

    @pl.when(j == 0)
    def _():
        gain = g_ref[...] * (1.0 + scl_ref[...])
        shift = shift_ref[...]

        def norm_rows(r, carry):
            rows = pl.ds(pl.multiple_of(r * NORM_ROWS, NORM_ROWS), NORM_ROWS)
            x = x_ref[rows, :]
            inv = lax.rsqrt(jnp.mean(x * x, axis=-1, keepdims=True) + EPS)
            h_ref[rows, :] = (x_ref[rows, :] * inv * gain + shift).astype(BF16)
            return carry

        lax.fori_loop(0, x_ref.shape[0] // NORM_ROWS, norm_rows, 0)

    def project():
        return jnp.dot(h_ref[...], w_ref[...], preferred_element_type=F32) + b_ref[...]

    gate_col = (j == COL_GA) | (j == COL_UG) | (j == COL_GC)

    @pl.when(gate_col)
    def _():
        acc = project()
        sig = _sigmoid(acc)
        z_ref[...] = jnp.where(j == COL_UG, sig, acc * sig).astype(BF16)

    @pl.when(jnp.logical_not(gate_col))
    def _():
        acc = project()
        if emit_kv:
            @pl.when(j == COL_K)
            def _():
                kc_ref[...] = acc.reshape(kc_ref.shape)

            @pl.when(j == COL_V)
            def _():
                vc_ref[...] = acc.reshape(vc_ref.shape)
        z_ref[...] = (acc * jnp.where(j == COL_Q, Q_SCALE, 1.0)).astype(BF16)


def _inproj(x, norm_g, mod, w_in, b_in, *, rows_per_mod, mod_row0, kv=None):
    t = x.shape[0]
    tm, tn = 1024, 1024
    tiles_per_mod = rows_per_mod // tm

    def mod_idx(part):
        return lambda i, j: ((mod_row0 + i // tiles_per_mod) * 3 + part, 0, 0)

    args = [x, norm_g.reshape(1, D_MODEL), mod, mod, w_in, b_in.reshape(1, D_IN)]
    x_mode = dict(pipeline_mode=pl.Buffered(1)) if kv is not None else {}
    in_specs = [
        pl.BlockSpec((tm, D_MODEL), lambda i, j: (i, 0), **x_mode),
        pl.BlockSpec((1, D_MODEL), lambda i, j: (0, 0)),
        pl.BlockSpec((None, 1, D_MODEL), mod_idx(0)),
        pl.BlockSpec((None, 1, D_MODEL), mod_idx(1)),
        pl.BlockSpec((D_MODEL, tn), lambda i, j: (0, j)),
        pl.BlockSpec((1, tn), lambda i, j: (0, j)),
    ]
    out_shape = [jax.ShapeDtypeStruct((t, D_IN), BF16)]
    out_specs = [pl.BlockSpec((tm, tn), lambda i, j: (i, j))]
    aliases = {}
    if kv is not None:
        layer, seq, caches = kv
        assert tm % seq == 0
        cache_shape = jax.ShapeDtypeStruct((t // seq, DEPTH, seq, D_ATTN), F32)
        cache_spec = pl.BlockSpec((tm // seq, None, seq, D_ATTN), lambda i, j: (i, layer, 0, 0))
        out_shape += [cache_shape, cache_shape]
        out_specs += [cache_spec, cache_spec]
        aliases = {len(args): 1, len(args) + 1: 2}
        args += list(caches)
        in_specs += [pl.BlockSpec(memory_space=pl.ANY)] * 2
    return pl.pallas_call(
        functools.partial(_inproj_kernel, emit_kv=kv is not None),
        grid=(t // tm, D_IN // tn),
        in_specs=in_specs,
        out_specs=out_specs,
        out_shape=out_shape,
        input_output_aliases=aliases,
        scratch_shapes=[pltpu.VMEM((tm, D_MODEL), BF16)],
        compiler_params=_params("arbitrary", "arbitrary"),
        name="inproj_kv" if kv is not None else "inproj",
    )(*args)


def _dot_nt(a, b):
    return lax.dot_general(a, b, (((1,), (1,)), ((), ())), preferred_element_type=F32)


def _ctx_attn_kernel(q_ref, k_ref, v_ref, ga_ref, o_ref):
    for h in range(N_HEADS_A):
        cols = slice(h * HEAD_DIM, (h + 1) * HEAD_DIM)
        s = _dot_nt(q_ref[:, cols], k_ref[:, cols])
        p = jnp.exp2(s - jnp.max(s, axis=-1, keepdims=True))
        denom = jnp.sum(p, axis=-1, keepdims=True)
        o = jnp.dot(p.astype(BF16), v_ref[:, cols], preferred_element_type=F32)
        o_ref[:, cols] = (o / denom * ga_ref[:, cols].astype(F32)).astype(BF16)


def _ctx_attention(z, seq):
    t = z.shape[0]

    def col(cb):
        return pl.BlockSpec((seq, D_ATTN), lambda b: (b, cb))

    return pl.pallas_call(
        _ctx_attn_kernel,
        grid=(t // seq,),
        in_specs=[col(COL_Q), col(COL_K), col(COL_V), col(COL_GA)],
        out_specs=pl.BlockSpec((seq, D_ATTN), lambda b: (b, 0)),
        out_shape=jax.ShapeDtypeStruct((t, D_ATTN), BF16),
        compiler_params=_params("arbitrary"),
        name="ctx_attention",
    )(z, z, z, z)


def _bias_kernel(toe_ref, o_ref, *, rows):
    n_groups = rows // Q_ROWS
    masked = jnp.full((GRID_W, GRID_W), MASKED, F32)
    for kind, g in enumerate((0, 1, n_groups - 1)):
        base = min(max(Q_ROWS * g - WIN_R // 2, 0), rows - K_ROWS)
        for a in range(Q_ROWS):
            i = Q_ROWS * g + a
            rs = min(max(i - WIN_R // 2, 0), rows - WIN_R)
            blocks = [toe_ref[base + t - i + WIN_R - 1] if rs <= base + t < rs + WIN_R else masked
                      for t in range(K_ROWS)]
            o_ref[kind, a * GRID_W:(a + 1) * GRID_W, :] = jnp.concatenate(blocks, axis=-1)


def _nbr_bias_table(rpb, rows):
    n_layers, n_heads, n_dr, _ = rpb.shape
    j = np.arange(GRID_W)[:, None]
    cc = np.arange(GRID_W)[None, :]
    cs = np.clip(j - WIN_C // 2, 0, GRID_W - WIN_C)
    valid_c = (cc >= cs) & (cc < cs + WIN_C)
    dc = cc - j + WIN_C - 1
    onehot = ((dc[None] == np.arange(2 * WIN_C - 1)[:, None, None]) & valid_c[None])
    toe = jnp.einsum('lhrd,djc->lhrjc', rpb, onehot.astype(np.float32),
                     precision=lax.Precision.HIGHEST)
    toe = jnp.where(valid_c, toe * LOG2E, MASKED).astype(F32)
    nq, nk = Q_ROWS * GRID_W, K_ROWS * GRID_W
    return pl.pallas_call(
        functools.partial(_bias_kernel, rows=rows),
        grid=(n_layers, n_heads),
        in_specs=[pl.BlockSpec((None, None, n_dr, GRID_W, GRID_W), lambda l, h: (l, h, 0, 0, 0))],
        out_specs=pl.BlockSpec((None, None, 3, nq, nk), lambda l, h: (l, h, 0, 0, 0)),
        out_shape=jax.ShapeDtypeStruct((n_layers, n_heads, 3, nq, nk), F32),
        compiler_params=_params("arbitrary", "arbitrary"),
        name="nbr_bias_table",
    )(toe)


def _nbr_attn_kernel(q_ref, k_ref, v_ref, ga_ref, ck_ref, cv_ref, bias_ref, o_ref,
                     kc_ref, vaug_ref, p_ref, *, rows):
    n_groups = rows // Q_ROWS
    nq = Q_ROWS * GRID_W
    nk = K_ROWS * GRID_W
    seq = rows * GRID_W
    past = ck_ref.shape[0]

    kc_ref[...] = ck_ref[...].astype(BF16)
    vaug_ref[0:seq, 0:HEAD_DIM] = v_ref[...]
    vaug_ref[seq:seq + past, 0:HEAD_DIM] = cv_ref[...].astype(BF16)
    vaug_ref[:, HEAD_DIM:] = jnp.ones((seq + past, HEAD_DIM), BF16)

    def slab_start(g):
        base = jnp.clip(Q_ROWS * g - WIN_R // 2, 0, rows - K_ROWS)
        return pl.multiple_of(base * GRID_W, GRID_W)

    def scores(g, slot):
        kind = jnp.where(g == 0, 0, jnp.where(g == n_groups - 1, 2, 1))
        q = q_ref[pl.ds(pl.multiple_of(g * nq, nq), nq), :]
        s_loc = _dot_nt(q, k_ref[pl.ds(slab_start(g), nk), :]) + bias_ref[kind]
        s_ctx = _dot_nt(q, kc_ref[...])
        m = jnp.maximum(jnp.max(s_loc, axis=-1, keepdims=True),
                        jnp.max(s_ctx, axis=-1, keepdims=True))
        p_ref[slot, :, 0:nk] = jnp.exp2(s_loc - m).astype(BF16)
        p_ref[slot, :, nk:] = jnp.exp2(s_ctx - m).astype(BF16)

    def values(g, slot):
        q0 = pl.multiple_of(g * nq, nq)
        o = (jnp.dot(p_ref[slot, :, 0:nk], vaug_ref[pl.ds(slab_start(g), nk), :],
                     preferred_element_type=F32)
             + jnp.dot(p_ref[slot, :, nk:], vaug_ref[seq:seq + past, :],
                       preferred_element_type=F32))
        gate = ga_ref[pl.ds(q0, nq), :].astype(F32)
        o_ref[pl.ds(q0, nq), :] = (o[:, :HEAD_DIM] / o[:, HEAD_DIM:] * gate).astype(BF16)

    def pair(k, score_slots, value_slots):
        scores(2 * k, score_slots[0])
        values(2 * k - 2, value_slots[0])
        scores(2 * k + 1, score_slots[1])
        values(2 * k - 1, value_slots[1])

    n_pairs = n_groups // 2
    scores(0, 0)
    scores(1, 1)

    def step(i, carry):
        pair(3 * i + 1, (2, 3), (0, 1))
        pair(3 * i + 2, (4, 5), (2, 3))
        pair(3 * i + 3, (0, 1), (4, 5))
        return carry

    lax.fori_loop(0, (n_pairs - 2) // 3, step, 0)
    pair(n_pairs - 1, (2, 3), (0, 1))
    values(n_groups - 2, 2)
    values(n_groups - 1, 3)


def _nbr_attention(z, cache_k, cache_v, bias, layer, seq):
    t = z.shape[0]
    rows = seq // GRID_W
    n_groups = rows // Q_ROWS
    assert n_groups % 2 == 0 and (n_groups // 2 - 2) % 3 == 0
    depth, past = cache_k.shape[1:3]
    cache_k = cache_k.reshape(t // seq, depth, past, D_ATTN)
    cache_v = cache_v.reshape(t // seq, depth, past, D_ATTN)

    def col(cb):
        return pl.BlockSpec((seq, HEAD_DIM), lambda h, b: (b, cb * N_HEADS_A + h))

    cache_spec = pl.BlockSpec((None, None, past, HEAD_DIM), lambda h, b: (b, layer, 0, h))
    return pl.pallas_call(
        functools.partial(_nbr_attn_kernel, rows=rows),
        grid=(N_HEADS_A, t // seq),
        in_specs=[col(COL_Q), col(COL_K), col(COL_V), col(COL_GA), cache_spec, cache_spec,
                  pl.BlockSpec((None, 3, Q_ROWS * GRID_W, K_ROWS * GRID_W),
                               lambda h, b: (h, 0, 0, 0))],
        out_specs=pl.BlockSpec((seq, HEAD_DIM), lambda h, b: (b, h)),
        out_shape=jax.ShapeDtypeStruct((t, D_ATTN), BF16),
        scratch_shapes=[pltpu.VMEM((past, HEAD_DIM), BF16),
                        pltpu.VMEM((seq + past, 2 * HEAD_DIM), BF16),
                        pltpu.VMEM((6, Q_ROWS * GRID_W, K_ROWS * GRID_W + past), BF16)],
        compiler_params=_params("arbitrary", "arbitrary"),
        name="nbr_attention",
    )(z, z, z, z, cache_k, cache_v, bias)


CONV_HALO = 16
CONV_CHUNK = 64


def _conv_kernel(u_ref, up_ref, un_ref, s_ref, sp_ref, sn_ref, gc_ref, dww_ref, dwb_ref,
                 lng_ref, lnb_ref, w_ref, b_ref, o_ref, ext_ref, conv_ref, *, tm, tiles_per_seq):
    i = pl.program_id(0)
    first = (i % tiles_per_seq) == 0
    last = (i % tiles_per_seq) == tiles_per_seq - 1
    halo_tiles = CONV_HALO // SUBLANES
    n_tiles = tm // SUBLANES

    def glu(u, s):
        return u[...].astype(F32) * s[...].astype(F32)

    def tiles(a):
        return a.reshape(a.shape[0] // SUBLANES, SUBLANES, D_CONV)

    ext_ref[0:halo_tiles] = tiles(jnp.where(first, 0.0, glu(up_ref, sp_ref)))
    ext_ref[halo_tiles:halo_tiles + n_tiles] = tiles(glu(u_ref, s_ref))
    ext_ref[halo_tiles + n_tiles:] = tiles(jnp.where(last, 0.0, glu(un_ref, sn_ref)))

    chunk_tiles = CONV_CHUNK // SUBLANES
    win_tiles = chunk_tiles + 2 * halo_tiles
    off = CONV_HALO - CONV_K // 2

    def chunk(cidx, carry):
        c0 = cidx * chunk_tiles
        for cb in range(D_CONV // LANES):
            lanes = slice(cb * LANES, (cb + 1) * LANES)
            win = ext_ref[pl.ds(c0, win_tiles), :, lanes].reshape(win_tiles * SUBLANES, LANES)
            acc = jnp.zeros((CONV_CHUNK, LANES), F32)
            for s in range(SUBLANES):
                taps = [k for k in range(CONV_K) if (k + off) % SUBLANES == s]
                if not taps:
                    continue
                sh = win if s == 0 else pltpu.roll(win, win_tiles * SUBLANES - s, axis=0)
                for k in taps:
                    a0 = k + off - s
                    acc = acc + sh[a0:a0 + CONV_CHUNK] * dww_ref[k:k + 1, lanes]
            conv_ref[pl.ds(c0, chunk_tiles), :, lanes] = acc.reshape(chunk_tiles, SUBLANES, LANES)
        return carry

    lax.fori_loop(0, tm // CONV_CHUNK, chunk, 0)

    a = conv_ref[...].reshape(tm, D_CONV) + dwb_ref[...]
    mu = jnp.mean(a, axis=-1, keepdims=True)
    d = a - mu
    var = jnp.mean(d * d, axis=-1, keepdims=True)
    y = d * lax.rsqrt(var + EPS) * lng_ref[...] + lnb_ref[...]
    y = y * _sigmoid(y)
    out = jnp.dot(y.astype(BF16), w_ref[...], preferred_element_type=F32) + b_ref[...]
    o_ref[...] = (out * gc_ref[...].astype(F32)).astype(BF16)


def _conformer_conv(z, dw_w, dw_b, ln_g, ln_b, w_pw2, b_pw2, *, seq, tm):
    t = z.shape[0]
    tiles_per_seq = seq // tm
    hb = tm // CONV_HALO
    n_hb = t // CONV_HALO

    def main(cb):
        return pl.BlockSpec((tm, D_CONV), lambda i: (i, cb))

    def before(cb):
        return pl.BlockSpec((CONV_HALO, D_CONV), lambda i: (jnp.maximum(i * hb - 1, 0), cb))

    def after(cb):
        return pl.BlockSpec((CONV_HALO, D_CONV),
                            lambda i: (jnp.minimum((i + 1) * hb, n_hb - 1), cb))

    def full(shape):
        return pl.BlockSpec(shape, lambda i: (0,) * len(shape))

    ext_tiles = (tm + 2 * CONV_HALO) // SUBLANES
    return pl.pallas_call(
        functools.partial(_conv_kernel, tm=tm, tiles_per_seq=tiles_per_seq),
        grid=(t // tm,),
        in_specs=[main(COL_U), before(COL_U), after(COL_U),
                  main(COL_UG), before(COL_UG), after(COL_UG), main(COL_GC),
                  full((CONV_K, D_CONV)), full((1, D_CONV)), full((1, D_CONV)), full((1, D_CONV)),
                  full((D_CONV, D_CONV)), full((1, D_CONV))],
        out_specs=pl.BlockSpec((tm, D_CONV), lambda i: (i, 0)),
        out_shape=jax.ShapeDtypeStruct((t, D_CONV), BF16),
        scratch_shapes=[pltpu.VMEM((ext_tiles, SUBLANES, D_CONV), F32),
                        pltpu.VMEM((tm // SUBLANES, SUBLANES, D_CONV), F32)],
        compiler_params=_params("arbitrary"),
        name="conformer_conv",
    )(z, z, z, z, z, z, z, dw_w, dw_b.reshape(1, D_CONV), ln_g.reshape(1, D_CONV),
      ln_b.reshape(1, D_CONV), w_pw2, b_pw2.reshape(1, D_CONV))


def _outproj_kernel(x_ref, a_ref, c_ref, gate_ref, w_ref, fg_ref, o_ref, cat_ref, *, final):
    cat_ref[:, :D_ATTN] = a_ref[...]
    cat_ref[:, D_ATTN:] = c_ref[...]
    out = jnp.dot(cat_ref[...], w_ref[...], preferred_element_type=F32)
    x = x_ref[...] + gate_ref[...] * out
    if final:
        x = x * lax.rsqrt(jnp.mean(x * x, axis=-1, keepdims=True) + EPS) * fg_ref[...]
    o_ref[...] = x


def _outproj(x, attn, conv, mod, w_out, final_g, *, rows_per_mod, mod_row0, final):
    t = x.shape[0]
    tm = 512
    tiles_per_mod = rows_per_mod // tm
    return pl.pallas_call(
        functools.partial(_outproj_kernel, final=final),
        grid=(t // tm,),
        in_specs=[
            pl.BlockSpec((tm, D_MODEL), lambda i: (i, 0)),
            pl.BlockSpec((tm, D_ATTN), lambda i: (i, 0)),
            pl.BlockSpec((tm, D_CONV), lambda i: (i, 0)),
            pl.BlockSpec((None, 1, D_MODEL),
                         lambda i: ((mod_row0 + i // tiles_per_mod) * 3 + 2, 0, 0)),
            pl.BlockSpec((D_MODEL, D_MODEL), lambda i: (0, 0)),
            pl.BlockSpec((1, D_MODEL), lambda i: (0, 0)),
        ],
        out_specs=pl.BlockSpec((tm, D_MODEL), lambda i: (i, 0)),
        out_shape=jax.ShapeDtypeStruct((t, D_MODEL), F32),
        scratch_shapes=[pltpu.VMEM((tm, D_MODEL), BF16)],
        compiler_params=_params("arbitrary"),
        name="outproj_final" if final else "outproj",
    )(x, attn, conv, mod, w_out, final_g.reshape(1, D_MODEL))


def kernel(x_prompt, x_sample, cache_k, cache_v, c, c_ctx, norm_g, w_ada, b_ada, w_in, b_in, rpb,
           dw_w, dw_b, cln_g, cln_b, w_pw2, b_pw2, w_out, final_norm_g):
    batch, seq, _ = x_prompt.shape
    dec_batch, dec_seq, _ = x_sample.shape
    assert dec_batch + 1 <= MOD_ROWS

    cvec = jnp.zeros((MOD_ROWS, D_MODEL), F32).at[:dec_batch].set(c).at[dec_batch].set(c_ctx)
    mod = _modulation(cvec, w_ada, b_ada).reshape(DEPTH, MOD_ROWS * 3, 1, D_MODEL)

    xp = x_prompt.reshape(batch * seq, D_MODEL)
    xs = x_sample.reshape(dec_batch * dec_seq, D_MODEL)
    caches = [jnp.zeros((batch, DEPTH, seq, D_ATTN), F32) for _ in range(2)]
    bias = _nbr_bias_table(rpb, dec_seq // GRID_W)
    for l in range(DEPTH):
        final = l == DEPTH - 1
        w_in_l, w_out_l = w_in[l].astype(BF16), w_out[l].astype(BF16)
        conv_w = (dw_w[l], dw_b[l], cln_g[l], cln_b[l], w_pw2[l].astype(BF16), b_pw2[l])
        prompt_mod = dict(rows_per_mod=batch * seq, mod_row0=dec_batch)
        sample_mod = dict(rows_per_mod=dec_seq, mod_row0=0)

        zp, *caches = _inproj(xp, norm_g[l], mod[l], w_in_l, b_in[l], kv=(l, seq, caches),
                              **prompt_mod)
        zs, = _inproj(xs, norm_g[l], mod[l], w_in_l, b_in[l], **sample_mod)

        ap = _ctx_attention(zp, seq)
        a_s = _nbr_attention(zs, cache_k, cache_v, bias[l], l, dec_seq)

        cp = _conformer_conv(zp, *conv_w, seq=seq, tm=seq)
        cs = _conformer_conv(zs, *conv_w, seq=dec_seq, tm=1024)

        xp = _outproj(xp, ap, cp, mod[l], w_out_l, final_norm_g, final=final, **prompt_mod)
        xs = _outproj(xs, a_s, cs, mod[l], w_out_l, final_norm_g, final=final, **sample_mod)

    y_prompt = xp.reshape(batch, seq, D_MODEL)
    y_sample = xs.reshape(dec_batch, dec_seq, D_MODEL)
    new_k, new_v = (a.reshape(batch, DEPTH, seq, N_HEADS_A, HEAD_DIM) for a in caches)
    return (y_prompt, y_sample, new_k, new_v)
```
